```python
import jax, jax.numpy as jnp
from jax import lax
import numpy as np

D_MODEL = 2048
BATCH = 4
SEQ = 4096
DEPTH = 2

CTX_LEN = 256
GRID_W = 64
N_MIXERS = 2
NORM_EPS = 1e-6
N_MOD = 6

MLSTM_HEADS = 8
MLSTM_QK_DIM = D_MODEL // 2 // MLSTM_HEADS
MLSTM_V_DIM = D_MODEL // MLSTM_HEADS
MLSTM_CHUNK = 64
GATE_SOFTCAP = 15.0
MLSTM_IN_COLS = 2 * MLSTM_HEADS * MLSTM_QK_DIM + 2 * MLSTM_HEADS * MLSTM_V_DIM + 4 * MLSTM_HEADS

ATTN_HEAD_DIM = 128
ATTN_Q_HEADS = D_MODEL // ATTN_HEAD_DIM
ATTN_KV_HEADS = 4
ATTN_QKV_COLS = (ATTN_Q_HEADS + 2 * ATTN_KV_HEADS) * ATTN_HEAD_DIM
Q_BLOCK = 128
ROPE_THETA = 10000.0

FFN_HIDDEN = -(-8 * D_MODEL // (3 * 256)) * 256

kernel_name = "hybrid_mlstm_gqa_prefix_dit"


def rms_norm(x, w):
    xf = x.astype(jnp.float32)
    y = xf * lax.rsqrt(jnp.mean(xf * xf, axis=-1, keepdims=True) + NORM_EPS)
    return (y * w.astype(jnp.float32)).astype(x.dtype)


def adaln(x, g, shift, scale):
    return rms_norm(x, g) * (1 + scale) + shift


def swiglu(h, w_in, w_out):
    gate, up = jnp.split(h @ w_in, 2, axis=-1)
    return (jax.nn.silu(gate) * up) @ w_out


def softcap(a, cap):
    return cap * jnp.tanh(a / cap)


def mlstm_project(h, w_in, b_gate):
    H, dk, dv = MLSTM_HEADS, MLSTM_QK_DIM, MLSTM_V_DIM
    p = h @ w_in
    B, T, _ = p.shape
    q, k, v, o, g = jnp.split(p, [H * dk, 2 * H * dk, 2 * H * dk + H * dv, 2 * H * dk + 2 * H * dv], axis=-1)
    q = q.reshape(B, T, H, dk).transpose(0, 2, 1, 3) * (dk ** -0.5)
    k = k.reshape(B, T, H, dk).transpose(0, 2, 1, 3)
    v = v.reshape(B, T, H, dv).transpose(0, 2, 1, 3)
    g = softcap(g.astype(jnp.float32) + b_gate.astype(jnp.float32), GATE_SOFTCAP)
    g = g.reshape(B, T, 4, H).transpose(2, 0, 3, 1)
    return q, k, v, o, g[0], jax.nn.log_sigmoid(g[1]), g[2], jax.nn.log_sigmoid(g[3])


def mlstm_scan(q, k, v, ig, lf, state):
    B, H, T, dk = q.shape
    dv = v.shape[-1]
    L = MLSTM_CHUNK
    nc = T // L

    def chunks(a):
        a = a.astype(jnp.float32).reshape(B, H, nc, L, *a.shape[3:])
        return jnp.moveaxis(a, 2, 0)

    lower = jnp.tril(jnp.ones((L, L), dtype=bool))

    def body(carry, inp):
        C, n, m = carry
        qc, kc, vc, ic, fc = inp
        b = jnp.cumsum(fc, axis=-1)
        log_d = b[..., :, None] - b[..., None, :] + ic[..., None, :]
        log_d = jnp.where(lower, log_d, -jnp.inf)
        log_inter = b + m[..., None]
        m_t = jnp.maximum(log_inter, jnp.max(log_d, axis=-1))
        dmat = jnp.exp(log_d - m_t[..., None])
        inter = jnp.exp(log_inter - m_t)
        sm = jnp.einsum('bhtd,bhsd->bhts', qc, kc) * dmat
        num = jnp.einsum('bhts,bhsv->bhtv', sm, vc) + inter[..., None] * jnp.einsum('bhtd,bhdv->bhtv', qc, C)
        den = jnp.sum(sm, axis=-1) + inter * jnp.einsum('bhtd,bhd->bht', qc, n)
        h = num / jnp.maximum(jnp.abs(den), jnp.exp(-m_t))[..., None]
        b_end = b[..., -1]
        log_w = b_end[..., None] - b + ic
        m_new = jnp.maximum(b_end + m, jnp.max(log_w, axis=-1))
        w = jnp.exp(log_w - m_new[..., None])
        decay = jnp.exp(b_end + m - m_new)
        C_new = decay[..., None, None] * C + jnp.einsum('bhs,bhsd,bhsv->bhdv', w, kc, vc)
        n_new = decay[..., None] * n + jnp.einsum('bhs,bhsd->bhd', w, kc)
        return (C_new, n_new, m_new), h

    state, hs = lax.scan(body, state, (chunks(q), chunks(k), chunks(v), chunks(ig), chunks(lf)))
    h = jnp.moveaxis(hs, 0, 2).reshape(B, H, T, dv)
    return state, h


def mlstm_out(h, o, g_head, w_out):
    B, H, T, dv = h.shape
    hn = h * lax.rsqrt(jnp.mean(h * h, axis=-1, keepdims=True) + NORM_EPS)
    hn = hn * g_head.astype(jnp.float32).reshape(1, H, 1, dv)
    hn = hn.transpose(0, 2, 1, 3).reshape(B, T, H * dv).astype(o.dtype)
    return (jax.nn.sigmoid(o) * hn) @ w_out


def mlstm_mixer(h_lat, h_ctx, w_in, b_gate, g_head, w_out, with_ctx_out):
    ql, kl, vl, ol, il_f, fl_f, il_b, fl_b = mlstm_project(h_lat, w_in, b_gate)
    qc, kc, vc, oc, ic_f, fc_f, ic_b, fc_b = mlstm_project(h_ctx, w_in, b_gate)
    B = h_lat.shape[0]
    H, dk, dv = MLSTM_HEADS, MLSTM_QK_DIM, MLSTM_V_DIM
    zero = (jnp.zeros((B, H, dk, dv), jnp.float32), jnp.zeros((B, H, dk), jnp.float32),
            jnp.zeros((B, H), jnp.float32))
    rev = lambda a: jnp.flip(a, axis=2)
    st_f, hc_f = mlstm_scan(qc, kc, vc, ic_f, fc_f, zero)
    st_b, hc_b = mlstm_scan(rev(qc), rev(kc), rev(vc), rev(ic_b), rev(fc_b), zero)
    _, hl_f = mlstm_scan(ql, kl, vl, il_f, fl_f, st_f)
    _, hl_b = mlstm_scan(rev(ql), rev(kl), rev(vl), rev(il_b), rev(fl_b), st_b)
    y_lat = mlstm_out(hl_f + rev(hl_b), ol, g_head, w_out)
    y_ctx = mlstm_out(hc_f + rev(hc_b), oc, g_head, w_out) if with_ctx_out else None
    return y_lat, y_ctx


def axial_rope_tables(n_tokens):
    rows = n_tokens // GRID_W
    t_row = jnp.repeat(jnp.arange(rows, dtype=jnp.float32), GRID_W)
    t_col = jnp.tile(jnp.arange(GRID_W, dtype=jnp.float32), rows)
    per_axis = ATTN_HEAD_DIM // 2
    inv = ROPE_THETA ** (-jnp.arange(0, per_axis, 2, dtype=jnp.float32) / per_axis)
    ang = jnp.concatenate([t_row[:, None] * inv, t_col[:, None] * inv], axis=-1)
    return jnp.cos(ang), jnp.sin(ang)


def apply_rope(x, cos, sin):
    xp = x.astype(jnp.float32).reshape(*x.shape[:-1], x.shape[-1] // 2, 2)
    x0, x1 = xp[..., 0], xp[..., 1]
    c = cos[None, :, None, :]
    s = sin[None, :, None, :]
    out = jnp.stack([x0 * c - x1 * s, x0 * s + x1 * c], axis=-1)
    return out.reshape(x.shape).astype(x.dtype)


def gqa_attend(q, k, v):
    s = jnp.einsum('bqkgd,bskd->bkgqs', q, k) * (ATTN_HEAD_DIM ** -0.5)
    p = jax.nn.softmax(s.astype(jnp.float32), axis=-1).astype(v.dtype)
    return jnp.einsum('bkgqs,bskd->bqkgd', p, v)


def attn_mixer(h_lat, h_ctx, w_qkv, g_q, g_k, w_out, with_ctx_out):
    Hq, Hkv, dh = ATTN_Q_HEADS, ATTN_KV_HEADS, ATTN_HEAD_DIM
    G = Hq // Hkv

    def qkv(h):
        p = h @ w_qkv
        B, T, _ = p.shape
        q, k, v = jnp.split(p, [Hq * dh, (Hq + Hkv) * dh], axis=-1)
        q = rms_norm(q.reshape(B, T, Hq, dh), g_q)
        k = rms_norm(k.reshape(B, T, Hkv, dh), g_k)
        return q, k, v.reshape(B, T, Hkv, dh)

    ql, kl, vl = qkv(h_lat)
    qc, kc, vc = qkv(h_ctx)
    B, T = h_lat.shape[:2]
    Lc = h_ctx.shape[1]
    cos, sin = axial_rope_tables(T)
    ql = apply_rope(ql, cos, sin)
    kl = apply_rope(kl, cos, sin)
    k_all = jnp.concatenate([kl, kc], axis=1)
    v_all = jnp.concatenate([vl, vc], axis=1)
    nb = T // Q_BLOCK
    qb = ql.reshape(B, nb, Q_BLOCK, Hkv, G, dh).transpose(1, 0, 2, 3, 4, 5)
    o_lat = lax.map(lambda qblk: gqa_attend(qblk, k_all, v_all), qb)
    o_lat = o_lat.transpose(1, 0, 2, 3, 4, 5).reshape(B, T, Hq * dh)
    y_lat = o_lat @ w_out
    if with_ctx_out:
        o_ctx = gqa_attend(qc.reshape(B, Lc, Hkv, G, dh), kc, vc).reshape(B, Lc, Hq * dh)
        y_ctx = o_ctx @ w_out
    else:
        y_ctx = None
    return y_lat, y_ctx


def setup_inputs(seed: int = 0) -> dict:
    key = jax.random.key(seed)
    ks = jax.random.split(key, 24)
    f32 = jnp.float32
    D, F = D_MODEL, FFN_HIDDEN
    n_a = (DEPTH + N_MIXERS - 1) // N_MIXERS
    n_b = DEPTH // N_MIXERS
    H = MLSTM_HEADS
    nrm = lambda k, shape, scale: jax.random.normal(k, shape, f32) * scale
    gain = lambda k, shape: 1.0 + 0.05 * jax.random.normal(k, shape, f32)
    fg_bias = jnp.linspace(3.0, 6.0, H, dtype=f32)
    gate_off = jnp.stack([jnp.zeros((H,), f32), fg_bias, jnp.zeros((H,), f32), fg_bias])
    b_gate = (0.1 * jax.random.normal(ks[10], (n_a, 4, H), f32) + gate_off).reshape(n_a, 4 * H)
    return {
        "x": jax.random.normal(ks[0], (BATCH, SEQ, D), f32),
        "c": jax.random.normal(ks[1], (BATCH, D), f32),
        "ctx": jax.random.normal(ks[2], (BATCH, CTX_LEN, D), f32),
        "c_ctx": jax.random.normal(ks[3], (D,), f32),
        "w_mod": nrm(ks[4], (DEPTH, D, N_MOD * D), 0.5 * D ** -0.5),
        "b_mod": nrm(ks[5], (DEPTH, N_MOD * D), 0.02),
        "g_mix_pre": gain(ks[6], (DEPTH, D)),
        "g_mix_post": gain(ks[7], (DEPTH, D)),
        "g_ffn_pre": gain(ks[8], (DEPTH, D)),
        "g_ffn_post": gain(ks[9], (DEPTH, D)),
        "w_mlstm_in": nrm(ks[11], (n_a, D, MLSTM_IN_COLS), D ** -0.5),
        "b_mlstm_gate": b_gate,
        "g_mlstm_head": gain(ks[12], (n_a, H * MLSTM_V_DIM)),
        "w_mlstm_out": nrm(ks[13], (n_a, H * MLSTM_V_DIM, D), (H * MLSTM_V_DIM) ** -0.5),
        "w_attn_qkv": nrm(ks[14], (n_b, D, ATTN_QKV_COLS), D ** -0.5),
        "g_attn_q": gain(ks[15], (n_b, ATTN_HEAD_DIM)),
        "g_attn_k": gain(ks[16], (n_b, ATTN_HEAD_DIM)),
        "w_attn_out": nrm(ks[17], (n_b, ATTN_Q_HEADS * ATTN_HEAD_DIM, D), (ATTN_Q_HEADS * ATTN_HEAD_DIM) ** -0.5),
        "w_ffn_in": nrm(ks[18], (DEPTH, D, 2 * F), D ** -0.5),
        "w_ffn_out": nrm(ks[19], (DEPTH, F, D), F ** -0.5),
    }


def reference(x, c, ctx, c_ctx, w_mod, b_mod, g_mix_pre, g_mix_post, g_ffn_pre, g_ffn_post,
              w_mlstm_in, b_mlstm_gate, g_mlstm_head, w_mlstm_out,
              w_attn_qkv, g_attn_q, g_attn_k, w_attn_out, w_ffn_in, w_ffn_out):
    cond_lat = jax.nn.silu(c)
    cond_ctx = jax.nn.silu(c_ctx)
    for i in range(DEPTH):
        last = i == DEPTH - 1
        j = i // N_MIXERS
        sh1, sc1, gt1, sh2, sc2, gt2 = jnp.split((cond_lat @ w_mod[i] + b_mod[i])[:, None, :], N_MOD, axis=-1)
        csh1, csc1, cgt1, csh2, csc2, cgt2 = jnp.split(cond_ctx @ w_mod[i] + b_mod[i], N_MOD, axis=-1)
        h_lat = adaln(x, g_mix_pre[i], sh1, sc1)
        h_ctx = adaln(ctx, g_mix_pre[i], csh1, csc1)
        if i % N_MIXERS == 0:
            y_lat, y_ctx = mlstm_mixer(h_lat, h_ctx, w_mlstm_in[j], b_mlstm_gate[j], g_mlstm_head[j],
                                       w_mlstm_out[j], not last)
        else:
            y_lat, y_ctx = attn_mixer(h_lat, h_ctx, w_attn_qkv[j], g_attn_q[j], g_attn_k[j],
                                      w_attn_out[j], not last)
        x = x + gt1 * rms_norm(y_lat, g_mix_post[i])
        f_lat = swiglu(adaln(x, g_ffn_pre[i], sh2, sc2), w_ffn_in[i], w_ffn_out[i])
        x = x + gt2 * rms_norm(f_lat, g_ffn_post[i])
        if not last:
            ctx = ctx + cgt1 * rms_norm(y_ctx, g_mix_post[i])
            f_ctx = swiglu(adaln(ctx, g_ffn_pre[i], csh2, csc2), w_ffn_in[i], w_ffn_out[i])
            ctx = ctx + cgt2 * rms_norm(f_ctx, g_ffn_post[i])
    return x
```

```python
import functools
import math

import jax
import jax.numpy as jnp
from jax import lax
from jax.experimental import pallas as pl
from jax.experimental.pallas import tpu as pltpu

F32 = jnp.float32
BF16 = jnp.bfloat16

D = 2048
BATCH = 4
SEQ = 4096
CTX = 256
N_LAT = BATCH * SEQ
N_CTX = BATCH * CTX
N_ALL = N_LAT + N_CTX
N_MOD = 6
EPS = 1e-6
MOD_ROWS = 8

MH = 8
MDK = 128
MDV = 256
MQK = MH * MDK
MV = MH * MDV
M_MAIN = 2 * MQK + 2 * MV
M_GATES = 4 * MH
GATE_PAD = 128
SOFTCAP = 15.0
M_SCALE_LOG = -0.5 * math.log(MDK)

AH = 16
AKV = 4
ADH = 128
AG = AH // AKV
A_SCALE = ADH ** -0.5
GRID_W = 64
ROPE_THETA = 10000.0

FFN = 5632

VMEM_LIMIT = 56 * 1024 * 1024
TM_PROJ = 512
TN_INPROJ = 2048
TN_QKV = 512
TM_OUT = 256
TM_FFN = 512
TF_FFN = 512
TN_MOD = 1024
L_SCAN = 256
TQ_ATT = 256
TK_ATT = 512

_NT = (((1,), (1,)), ((), ()))
_TN = (((0,), (0,)), ((), ()))


def _params(n_axes):
    return pltpu.CompilerParams(
        dimension_semantics=("arbitrary",) * n_axes, vmem_limit_bytes=VMEM_LIMIT)


def _mod_row(i, tm):
    return jnp.where(i < N_LAT // tm, i // (SEQ // tm), BATCH)


def _mod_spec(tm, chunk):
    return pl.BlockSpec((None, 1, D), lambda i, j: (_mod_row(i, tm), 0, chunk))


def _adaln(x, gain, shift, scale):
    r = lax.rsqrt(jnp.mean(x * x, axis=-1, keepdims=True) + EPS)
    return (x * r) * (gain * (1.0 + scale)) + shift


def _post_norm_residual(x, y, gain, gate):
    r = lax.rsqrt(jnp.mean(y * y, axis=-1, keepdims=True) + EPS)
    return x + gate * ((y * r) * gain)


def _sigmoid(z):
    return 1.0 / (1.0 + jnp.exp(-z))


def _mod_kernel(c_ref, w_ref, b_ref, o_ref):
    c = c_ref[...]
    cond = (c * _sigmoid(c)).astype(BF16)
    o_ref[...] = jnp.dot(cond, w_ref[...].astype(BF16), preferred_element_type=F32) + b_ref[...]


def _modulation(c_all, w_mod, b_mod):
    depth = w_mod.shape[0]
    return pl.pallas_call(
        _mod_kernel,
        out_shape=jax.ShapeDtypeStruct((depth, MOD_ROWS, N_MOD * D), F32),
        grid=(depth, N_MOD * D // TN_MOD),
        in_specs=[
            pl.BlockSpec((MOD_ROWS, D), lambda l, j: (0, 0)),
            pl.BlockSpec((None, D, TN_MOD), lambda l, j: (l, 0, j)),
            pl.BlockSpec((None, 1, TN_MOD), lambda l, j: (l, 0, j)),
        ],
        out_specs=pl.BlockSpec((None, MOD_ROWS, TN_MOD), lambda l, j: (l, 0, j)),
        compiler_params=_params(2),
        name="modulation",
    )(c_all, w_mod, b_mod.reshape(depth, 1, N_MOD * D))


def _inproj_kernel(xl_ref, xc_ref, g_ref, sh_ref, sc_ref, w_ref, wg_ref, p_ref, gpre_ref, h_scr):
    i = pl.program_id(0)
    j = pl.program_id(1)

    @pl.when(j == 0)
    def _():
        x = jnp.where(i < N_LAT // TM_PROJ, xl_ref[...], xc_ref[...])
        h_scr[...] = _adaln(x, g_ref[...], sh_ref[...], sc_ref[...]).astype(BF16)
        gpre_ref[...] = jnp.dot(h_scr[...], wg_ref[...], preferred_element_type=F32)

    p_ref[...] = jnp.dot(h_scr[...], w_ref[...], preferred_element_type=F32).astype(BF16)


def _two_source_specs(tm):
    n_lat = N_LAT // tm
    return [
        pl.BlockSpec((tm, D), lambda i, *_: (jnp.minimum(i, n_lat - 1), 0)),
        pl.BlockSpec((tm, D), lambda i, *_: (jnp.maximum(i - n_lat, 0), 0)),
    ]


def _mlstm_inproj(xl, xc, gain, mod, w_main, w_gate):
    tm, tn = TM_PROJ, TN_INPROJ
    vec = pl.BlockSpec((1, D), lambda i, j: (0, 0))
    return pl.pallas_call(
        _inproj_kernel,
        out_shape=(jax.ShapeDtypeStruct((N_ALL, M_MAIN), BF16),
                   jax.ShapeDtypeStruct((N_ALL, GATE_PAD), F32)),
        grid=(N_ALL // tm, M_MAIN // tn),
        in_specs=_two_source_specs(tm) + [
            vec, _mod_spec(tm, 0), _mod_spec(tm, 1),
            pl.BlockSpec((D, tn), lambda i, j: (0, j)),
            pl.BlockSpec((D, GATE_PAD), lambda i, j: (0, 0)),
        ],
        out_specs=(pl.BlockSpec((tm, tn), lambda i, j: (i, j)),
                   pl.BlockSpec((tm, GATE_PAD), lambda i, j: (i, 0))),
        scratch_shapes=[pltpu.VMEM((tm, D), BF16)],
        compiler_params=_params(2),
        name="mlstm_inproj",
    )(xl, xc, gain, mod, mod, w_main, w_gate)


def _gateprep_kernel(gpre_ref, b_ref, gc_ref, gr_ref):
    L = gpre_ref.shape[0]
    z = gpre_ref[...] + b_ref[...]
    a = SOFTCAP * jnp.tanh(z * (1.0 / SOFTCAP))
    logsig = jnp.minimum(a, 0.0) - jnp.log(1.0 + jnp.exp(-jnp.abs(a)))
    row = lax.broadcasted_iota(jnp.int32, (L, GATE_PAD), 0)
    lane = lax.broadcasted_iota(jnp.int32, (L, GATE_PAD), 1)
    pre = logsig
    suf = logsig
    s = 1
    while s < L:
        pre = pre + jnp.where(row >= s, pltpu.roll(pre, s, 0), 0.0)
        suf = suf + jnp.where(row < L - s, pltpu.roll(suf, L - s, 0), 0.0)
        s *= 2
    out = jnp.where((lane >= MH) & (lane < 2 * MH), pre,
                    jnp.where((lane >= 3 * MH) & (lane < 4 * MH), suf, a))
    gc_ref[...] = out
    gr_ref[...] = out.T[:M_GATES, :]


def _gateprep(gpre, bias):
    L = L_SCAN
    return pl.pallas_call(
        _gateprep_kernel,
        out_shape=(jax.ShapeDtypeStruct((N_ALL, GATE_PAD), F32),
                   jax.ShapeDtypeStruct((M_GATES, N_ALL), F32)),
        grid=(N_ALL // L,),
        in_specs=[pl.BlockSpec((L, GATE_PAD), lambda i: (i, 0)),
                  pl.BlockSpec((1, GATE_PAD), lambda i: (0, 0))],
        out_specs=(pl.BlockSpec((L, GATE_PAD), lambda i: (i, 0)),
                   pl.BlockSpec((M_GATES, L), lambda i: (0, i))),
        compiler_params=_params(1),
        name="mlstm_gateprep",
    )(gpre, bias)


def _scan_unit(q, k, v, b_col, i_col, b_row, i_row, b_end, mask, c_ref, n_ref, m_ref, h_out):
    m = m_ref[0:1, 0:1]
    n = n_ref[0:1, :]
    C = c_ref[...]
    log_d = jnp.where(mask, b_col + (i_row - b_row), -jnp.inf)
    log_inter = b_col + m
    m_t = jnp.maximum(log_inter, jnp.max(log_d, axis=-1, keepdims=True))
    m_ts = m_t - M_SCALE_LOG
    dmat = jnp.exp(log_d - m_ts)
    inter = jnp.exp(log_inter - m_ts)
    sm = lax.dot_general(q, k, _NT, preferred_element_type=F32) * dmat
    q_c = jnp.dot(q, C.astype(BF16), preferred_element_type=F32)
    num = jnp.dot(sm.astype(BF16), v, preferred_element_type=F32) + inter * q_c
    q_n = jnp.sum(q.astype(F32) * n, axis=-1, keepdims=True)
    den = jnp.sum(sm, axis=-1, keepdims=True) + inter * q_n
    h_out[...] = num * (1.0 / jnp.maximum(jnp.abs(den), jnp.exp(-m_t)))

    log_w = b_end - b_col + i_col
    m_new = jnp.maximum(b_end + m, jnp.max(log_w, axis=0, keepdims=True))
    w = jnp.exp(log_w - m_new)
    decay = jnp.exp(b_end + m - m_new)
    wv = (w * v.astype(F32)).astype(BF16)
    c_ref[...] = decay * C + lax.dot_general(k, wv, _TN, preferred_element_type=F32)
    n_ref[0:1, :] = decay * n + jnp.sum(w * k.astype(F32), axis=0, keepdims=True)
    m_ref[...] = jnp.broadcast_to(m_new, m_ref.shape)


def _scan_kernel(qf, kf, vf, gcf, grf, qb, kb, vb, gcb, grb, hf_ref, hb_ref, c_scr, n_scr, m_scr):
    L = qf.shape[0]

    @pl.when(pl.program_id(1) == 0)
    def _():
        c_scr[...] = jnp.zeros_like(c_scr)
        n_scr[...] = jnp.zeros_like(n_scr)
        m_scr[...] = jnp.zeros_like(m_scr)

    row = lax.broadcasted_iota(jnp.int32, (L, L), 0)
    col = lax.broadcasted_iota(jnp.int32, (L, L), 1)
    directions = (
        (qf, kf, vf, gcf, grf, hf_ref, col <= row, 0, L - 1),
        (qb, kb, vb, gcb, grb, hb_ref, col >= row, 2 * MH, 0),
    )
    for d, (q_ref, k_ref, v_ref, gc_ref, gr_ref, h_ref, mask, goff, end) in enumerate(directions):
        for h in range(MH):
            ig, bc = goff + h, goff + MH + h
            b_row = gr_ref[bc:bc + 1, :]
            _scan_unit(
                q_ref[:, h * MDK:(h + 1) * MDK], k_ref[:, h * MDK:(h + 1) * MDK],
                v_ref[:, h * MDV:(h + 1) * MDV],
                gc_ref[:, bc:bc + 1], gc_ref[:, ig:ig + 1], b_row, gr_ref[ig:ig + 1, :],
                b_row[:, end:end + 1], mask,
                c_scr.at[d * MH + h], n_scr.at[d * MH + h], m_scr.at[d * MH + h],
                h_ref.at[:, h * MDV:(h + 1) * MDV])


def _mlstm_scan(p, gc, gr):
    L = L_SCAN
    n_ctx_chunks = CTX // L
    n_lat_chunks = SEQ // L
    steps = n_ctx_chunks + n_lat_chunks
    ctx0 = N_LAT // L

    def fwd(b, s):
        return jnp.where(s < n_ctx_chunks, ctx0 + b * n_ctx_chunks + s,
                         b * n_lat_chunks + s - n_ctx_chunks)

    def bwd(b, s):
        return jnp.where(s < n_ctx_chunks, ctx0 + b * n_ctx_chunks + (n_ctx_chunks - 1 - s),
                         b * n_lat_chunks + (n_lat_chunks - 1 - (s - n_ctx_chunks)))

    def specs(idx):
        return [
            pl.BlockSpec((L, MQK), lambda b, s: (idx(b, s), 0)),
            pl.BlockSpec((L, MQK), lambda b, s: (idx(b, s), 1)),
            pl.BlockSpec((L, MV), lambda b, s: (idx(b, s), 1)),
            pl.BlockSpec((L, GATE_PAD), lambda b, s: (idx(b, s), 0)),
            pl.BlockSpec((M_GATES, L), lambda b, s: (0, idx(b, s))),
        ]

    return pl.pallas_call(
        _scan_kernel,
        out_shape=(jax.ShapeDtypeStruct((N_ALL, MV), F32),
                   jax.ShapeDtypeStruct((N_ALL, MV), F32)),
        grid=(BATCH, steps),
        in_specs=specs(fwd) + specs(bwd),
        out_specs=(pl.BlockSpec((L, MV), lambda b, s: (fwd(b, s), 0)),
                   pl.BlockSpec((L, MV), lambda b, s: (bwd(b, s), 0))),
        scratch_shapes=[pltpu.VMEM((2 * MH, MDK, MDV), F32),
                        pltpu.VMEM((2 * MH, 8, MDK), F32),
                        pltpu.VMEM((2 * MH, 8, 128), F32)],
        compiler_params=_params(2),
        name="mlstm_scan",
    )(p, p, p, gc, gr, p, p, p, gc, gr)


def _mlstm_out_kernel(hf_ref, hb_ref, o_ref, xl_ref, xc_ref, gh_ref, gp_ref, gt_ref, w_ref,
                      out_ref, a_scr):
    i = pl.program_id(0)
    for h in range(MH):
        sl = slice(h * MDV, (h + 1) * MDV)
        hh = hf_ref[:, sl] + hb_ref[:, sl]
        r = lax.rsqrt(jnp.mean(hh * hh, axis=-1, keepdims=True) + EPS)
        hn = (hh * r) * gh_ref[:, sl]
        a_scr[:, sl] = (_sigmoid(o_ref[:, sl].astype(F32)) * hn).astype(BF16)
    y = jnp.dot(a_scr[...], w_ref[...], preferred_element_type=F32)
    x = jnp.where(i < N_LAT // TM_OUT, xl_ref[...], xc_ref[...])
    out_ref[...] = _post_norm_residual(x, y, gp_ref[...], gt_ref[...])


def _mlstm_out(hf, hb, p, xl, xc, g_head, g_post, mod, w_out):
    tm = TM_OUT
    vec = pl.BlockSpec((1, D), lambda i: (0, 0))
    return pl.pallas_call(
        _mlstm_out_kernel,
        out_shape=jax.ShapeDtypeStruct((N_ALL, D), F32),
        grid=(N_ALL // tm,),
        in_specs=[
            pl.BlockSpec((tm, MV), lambda i: (i, 0)),
            pl.BlockSpec((tm, MV), lambda i: (i, 0)),
            pl.BlockSpec((tm, MV), lambda i: (i, 2)),
        ] + _two_source_specs(tm) + [
            vec, vec,
            pl.BlockSpec((None, 1, D), lambda i: (_mod_row(i, tm), 0, 2)),
            pl.BlockSpec((MV, D), lambda i: (0, 0)),
        ],
        out_specs=pl.BlockSpec((tm, D), lambda i: (i, 0)),
        scratch_shapes=[pltpu.VMEM((tm, MV), BF16)],
        compiler_params=_params(1),
        name="mlstm_out",
    )(hf, hb, p, xl, xc, g_head, g_post, mod, w_out)


def _ffn_kernel(x_ref, g_ref, sh_ref, sc_ref, gt_ref, gp_ref, wg_ref, wu_ref, wo_ref, out_ref,
                h_scr, acc_scr):
    f = pl.program_id(1)

    @pl.when(f == 0)
    def _():
        h_scr[...] = _adaln(x_ref[...], g_ref[...], sh_ref[...], sc_ref[...]).astype(BF16)
        acc_scr[...] = jnp.zeros_like(acc_scr)

    h = h_scr[...]
    gate = jnp.dot(h, wg_ref[...], preferred_element_type=F32)
    up = jnp.dot(h, wu_ref[...], preferred_element_type=F32)
    act = ((gate * _sigmoid(gate)) * up).astype(BF16)
    acc_scr[...] += jnp.dot(act, wo_ref[...], preferred_element_type=F32)

    @pl.when(f == pl.num_programs(1) - 1)
    def _():
        out_ref[...] = _post_norm_residual(x_ref[...], acc_scr[...], gp_ref[...], gt_ref[...])


def _ffn(x_rows, g_pre, g_post, mod, w_in, w_out):
    n_rows = x_rows.shape[0]
    tm, tf = TM_FFN, TF_FFN
    nf = FFN // tf
    vec = pl.BlockSpec((1, D), lambda i, f: (0, 0))
    return pl.pallas_call(
        _ffn_kernel,
        out_shape=jax.ShapeDtypeStruct((n_rows, D), F32),
        grid=(n_rows // tm, nf),
        in_specs=[
            pl.BlockSpec((tm, D), lambda i, f: (i, 0)),
            vec, _mod_spec(tm, 3), _mod_spec(tm, 4), _mod_spec(tm, 5), vec,
            pl.BlockSpec((D, tf), lambda i, f: (0, f)),
            pl.BlockSpec((D, tf), lambda i, f: (0, nf + f)),
            pl.BlockSpec((tf, D), lambda i, f: (f, 0)),
        ],
        out_specs=pl.BlockSpec((tm, D), lambda i, f: (i, 0)),
        scratch_shapes=[pltpu.VMEM((tm, D), BF16), pltpu.VMEM((tm, D), F32)],
        compiler_params=_params(2),
        name="ffn",
    )(x_rows, g_pre, mod, mod, mod, g_post, w_in, w_in, w_out)


def _norm_rope(y, gain, cos, sin, scale):
    even = (lax.broadcasted_iota(jnp.int32, (y.shape[0], ADH), 1) & 1) == 0
    outs = []
    for h in range(y.shape[1] // ADH):
        yy = y[:, h * ADH:(h + 1) * ADH]
        r = lax.rsqrt(jnp.mean(yy * yy, axis=-1, keepdims=True) + EPS)
        yn = (yy * r) * gain
        partner = jnp.where(even, pltpu.roll(yn, ADH - 1, 1), pltpu.roll(yn, 1, 1))
        outs.append(((yn * cos + partner * sin) * scale).astype(BF16))
    return jnp.concatenate(outs, axis=1)


def _qkv_kernel(x_ref, g_ref, sh_ref, sc_ref, gq_ref, gk_ref, cos_ref, sin_ref, w_ref,
                q_ref, k_ref, v_ref, h_scr):
    j = pl.program_id(1)
    n_q = AH * ADH // TN_QKV

    @pl.when(j == 0)
    def _():
        h_scr[...] = _adaln(x_ref[...], g_ref[...], sh_ref[...], sc_ref[...]).astype(BF16)

    y = jnp.dot(h_scr[...], w_ref[...], preferred_element_type=F32)

    @pl.when(j < n_q)
    def _():
        q_ref[...] = _norm_rope(y, gq_ref[...], cos_ref[...], sin_ref[...], A_SCALE)

    @pl.when(j == n_q)
    def _():
        k_ref[...] = _norm_rope(y, gk_ref[...], cos_ref[...], sin_ref[...], 1.0)

    @pl.when(j == n_q + 1)
    def _():
        v_ref[...] = y.astype(BF16)


def _attn_qkv(x_all, gain, mod, g_q, g_k, cos_t, sin_t, w_qkv):
    tm, tn = TM_PROJ, TN_QKV
    n_q = AH * ADH // tn
    n_lat = N_LAT // tm
    per_seq = SEQ // tm
    vec = pl.BlockSpec((1, D), lambda i, j: (0, 0))
    hvec = pl.BlockSpec((1, ADH), lambda i, j: (0, 0))
    rope = pl.BlockSpec((tm, ADH), lambda i, j: (jnp.where(i < n_lat, i % per_seq, per_seq), 0))
    kv_cols = AKV * ADH
    return pl.pallas_call(
        _qkv_kernel,
        out_shape=(jax.ShapeDtypeStruct((N_ALL, AH * ADH), BF16),
                   jax.ShapeDtypeStruct((N_ALL, kv_cols), BF16),
                   jax.ShapeDtypeStruct((N_ALL, kv_cols), BF16)),
        grid=(N_ALL // tm, n_q + 2),
        in_specs=[
            pl.BlockSpec((tm, D), lambda i, j: (i, 0)),
            vec, _mod_spec(tm, 0), _mod_spec(tm, 1), hvec, hvec, rope, rope,
            pl.BlockSpec((D, tn), lambda i, j: (0, j)),
        ],
        out_specs=(pl.BlockSpec((tm, tn), lambda i, j: (i, jnp.minimum(j, n_q - 1))),
                   pl.BlockSpec((tm, kv_cols), lambda i, j: (i, 0)),
                   pl.BlockSpec((tm, kv_cols), lambda i, j: (i, 0))),
        scratch_shapes=[pltpu.VMEM((tm, D), BF16)],
        compiler_params=_params(2),
        name="attn_qkv",
    )(x_all, gain, mod, mod, g_q, g_k, cos_t, sin_t, w_qkv)


def _attn_kernel(q_ref, kl_ref, kc_ref, vl_ref, vc_ref, o_ref, m_scr, l_scr, acc_scr):
    tq = q_ref.shape[0]
    q = jnp.concatenate([q_ref[:, g * ADH:(g + 1) * ADH] for g in range(AG)], axis=0)
    m_scr[...] = jnp.full_like(m_scr, -jnp.inf)
    l_scr[...] = jnp.zeros_like(l_scr)
    acc_scr[...] = jnp.zeros_like(acc_scr)

    def block(k, v):
        s = lax.dot_general(q, k, _NT, preferred_element_type=F32)
        m_old = m_scr[...]
        m_new = jnp.maximum(m_old, jnp.max(s, axis=-1, keepdims=True))
        alpha = jnp.exp(m_old - m_new)
        p = jnp.exp(s - m_new)
        l_scr[...] = alpha * l_scr[...] + jnp.sum(p, axis=-1, keepdims=True)
        acc_scr[...] = alpha * acc_scr[...] + jnp.dot(p.astype(BF16), v, preferred_element_type=F32)
        m_scr[...] = m_new

    def body(c, carry):
        rows = pl.ds(pl.multiple_of(c * TK_ATT, TK_ATT), TK_ATT)
        block(kl_ref[rows, :], vl_ref[rows, :])
        return carry

    lax.fori_loop(0, SEQ // TK_ATT, body, 0)
    block(kc_ref[...], vc_ref[...])
    out = (acc_scr[...] * (1.0 / l_scr[...])).astype(BF16)
    for g in range(AG):
        o_ref[:, g * ADH:(g + 1) * ADH] = out[g * tq:(g + 1) * tq, :]


def _attention(q, k, v):
    tq = TQ_ATT
    per_seq = SEQ // tq
    return pl.pallas_call(
        _attn_kernel,
        out_shape=jax.ShapeDtypeStruct((N_LAT, AH * ADH), BF16),
        grid=(BATCH, AKV, per_seq),
        in_specs=[
            pl.BlockSpec((tq, AG * ADH), lambda b, h, i: (b * per_seq + i, h)),
            pl.BlockSpec((SEQ, ADH), lambda b, h, i: (b, h)),
            pl.BlockSpec((CTX, ADH), lambda b, h, i: (N_LAT // CTX + b, h)),
            pl.BlockSpec((SEQ, ADH), lambda b, h, i: (b, h)),
            pl.BlockSpec((CTX, ADH), lambda b, h, i: (N_LAT // CTX + b, h)),
        ],
        out_specs=pl.BlockSpec((tq, AG * ADH), lambda b, h, i: (b * per_seq + i, h)),
        scratch_shapes=[pltpu.VMEM((AG * tq, 1), F32), pltpu.VMEM((AG * tq, 1), F32),
                        pltpu.VMEM((AG * tq, ADH), F32)],
        compiler_params=_params(3),
        name="attention",
    )(q, k, k, v, v)


def _attn_out_kernel(o_ref, x_ref, gp_ref, gt_ref, w_ref, out_ref):
    y = jnp.dot(o_ref[...], w_ref[...], preferred_element_type=F32)
    out_ref[...] = _post_norm_residual(x_ref[...], y, gp_ref[...], gt_ref[...])


def _attn_out(o, x_all, g_post, mod, w_out):
    tm = TM_OUT
    return pl.pallas_call(
        _attn_out_kernel,
        out_shape=jax.ShapeDtypeStruct((N_LAT, D), F32),
        grid=(N_LAT // tm,),
        in_specs=[
            pl.BlockSpec((tm, D), lambda i: (i, 0)),
            pl.BlockSpec((tm, D), lambda i: (i, 0)),
            pl.BlockSpec((1, D), lambda i: (0, 0)),
            pl.BlockSpec((None, 1, D), lambda i: (_mod_row(i, tm), 0, 2)),
            pl.BlockSpec((D, D), lambda i: (0, 0)),
        ],
        out_specs=pl.BlockSpec((tm, D), lambda i: (i, 0)),
        compiler_params=_params(1),
        name="attn_out",
    )(o, x_all, g_post, mod, w_out)


def _rope_tables():
    rows = SEQ // GRID_W
    t_row = jnp.repeat(jnp.arange(rows, dtype=F32), GRID_W)
    t_col = jnp.tile(jnp.arange(GRID_W, dtype=F32), rows)
    per_axis = ADH // 2
    inv = ROPE_THETA ** (-jnp.arange(0, per_axis, 2, dtype=F32) / per_axis)
    ang = jnp.concatenate([t_row[:, None] * inv, t_col[:, None] * inv], axis=-1)
    cos = jnp.repeat(jnp.cos(ang), 2, axis=-1)
    sin = jnp.repeat(jnp.sin(ang), 2, axis=-1) * jnp.tile(jnp.array([-1.0, 1.0], F32), per_axis)
    pad = TM_PROJ
    cos = jnp.concatenate([cos, jnp.ones((pad, ADH), F32)], axis=0)
    sin = jnp.concatenate([sin, jnp.zeros((pad, ADH), F32)], axis=0)
    return cos, sin


def kernel(x, c, ctx, c_ctx, w_mod, b_mod, g_mix_pre, g_mix_post, g_ffn_pre, g_ffn_post,
           w_mlstm_in, b_mlstm_gate, g_mlstm_head, w_mlstm_out,
           w_attn_qkv, g_attn_q, g_attn_k, w_attn_out, w_ffn_in, w_ffn_out):
    assert x.shape == (BATCH, SEQ, D) and ctx.shape == (BATCH, CTX, D)
    xl = x.reshape(N_LAT, D)
    xc = ctx.reshape(N_CTX, D)
    row = lambda a: a.reshape(1, -1)

    c_all = jnp.concatenate([c, c_ctx[None, :], jnp.zeros((MOD_ROWS - BATCH - 1, D), F32)], axis=0)
    mod = _modulation(c_all, w_mod, b_mod).reshape(2, MOD_ROWS, 1, N_MOD * D)

    w_in = w_mlstm_in[0]
    w_main = w_in[:, :M_MAIN].astype(BF16)
    w_gate = jnp.pad(w_in[:, M_MAIN:], ((0, 0), (0, GATE_PAD - M_GATES))).astype(BF16)
    b_gate = jnp.pad(b_mlstm_gate[0], (0, GATE_PAD - M_GATES)).reshape(1, GATE_PAD)
    p, gpre = _mlstm_inproj(xl, xc, row(g_mix_pre[0]), mod[0], w_main, w_gate)
    gc, gr = _gateprep(gpre, b_gate)
    hf, hb = _mlstm_scan(p, gc, gr)
    x_all = _mlstm_out(hf, hb, p, xl, xc, row(g_mlstm_head[0]), row(g_mix_post[0]), mod[0],
                       w_mlstm_out[0].astype(BF16))
    x_all = _ffn(x_all, row(g_ffn_pre[0]), row(g_ffn_post[0]), mod[0],
                 w_ffn_in[0].astype(BF16), w_ffn_out[0].astype(BF16))

    cos_t, sin_t = _rope_tables()
    q, k, v = _attn_qkv(x_all, row(g_mix_pre[1]), mod[1], row(g_attn_q[0]), row(g_attn_k[0]),
                        cos_t, sin_t, w_attn_qkv[0].astype(BF16))
    o = _attention(q, k, v)
    x_lat = _attn_out(o, x_all, row(g_mix_post[1]), mod[1], w_attn_out[0].astype(BF16))
    x_lat = _ffn(x_lat, row(g_ffn_pre[1]), row(g_ffn_post[1]), mod[1],
                 w_ffn_in[1].astype(BF16), w_ffn_out[1].astype(BF16))
    return x_lat.reshape(BATCH, SEQ, D)
```

```python
import functools
import math

import jax
import jax.numpy as jnp
from jax import lax
from jax.experimental import pallas as pl
from jax.experimental.pallas import tpu as pltpu

F32 = jnp.float32
BF16 = jnp.bfloat16

D = 2048
BATCH = 4
SEQ = 4096
CTX = 256
N_LAT = BATCH * SEQ
N_CTX = BATCH * CTX
N_ALL = N_LAT + N_CTX
N_MOD = 6
EPS = 1e-6
MOD_ROWS = 8

MH = 8
MDK = 128
MDV = 256
MQK = MH * MDK
MV = MH * MDV
M_MAIN = 2 * MQK + 2 * MV
M_GATES = 4 * MH
GATE_PAD = 128
SOFTCAP = 15.0
M_SCALE_LOG = -0.5 * math.log(MDK)

AH = 16
AKV = 4
ADH = 128
AG = AH // AKV
A_SCALE = ADH ** -0.5
VT_ONES = 16
VT_ROWS = ADH + VT_ONES
GRID_W = 64
ROPE_THETA = 10000.0

FFN = 5632

VMEM_LIMIT = 56 * 1024 * 1024
TM_PROJ = 512
TN_INPROJ = 2048
TN_QKV = 1024
TM_OUT = 256
TM_FFN = 512
TF_FFN = 512
TN_MOD = 1024
L_SCAN = 256
TQ_ATT = 256
TK_ATT = 512

_NT = (((1,), (1,)), ((), ()))
_TN = (((0,), (0,)), ((), ()))


def _params(n_axes):
    return pltpu.CompilerParams(
        dimension_semantics=("arbitrary",) * n_axes, vmem_limit_bytes=VMEM_LIMIT)


def _mod_row(i, tm):
    return jnp.where(i < N_LAT // tm, i // (SEQ // tm), BATCH)


def _mod_spec(tm, chunk):
    return pl.BlockSpec((None, 1, D), lambda i, j: (_mod_row(i, tm), 0, chunk))


def _adaln(x, gain, shift, scale):
    r = lax.rsqrt(jnp.mean(x * x, axis=-1, keepdims=True) + EPS)
    return (x * r) * (gain * (1.0 + scale)) + shift


def _post_norm_residual(x, y, gain, gate):
    r = lax.rsqrt(jnp.mean(y * y, axis=-1, keepdims=True) + EPS)
    return x + gate * ((y * r) * gain)


def _sigmoid(z):
    return 1.0 / (1.0 + jnp.exp(-z))


def _mod_kernel(c_ref, w_ref, b_ref, o_ref):
    c = c_ref[...]
    cond = (c * _sigmoid(c)).astype(BF16)
    o_ref[...] = jnp.dot(cond, w_ref[...].astype(BF16), preferred_element_type=F32) + b_ref[...]


def _modulation(c_all, w_mod, b_mod):
    depth = w_mod.shape[0]
    return pl.pallas_call(
        _mod_kernel,
        out_shape=jax.ShapeDtypeStruct((depth, MOD_ROWS, N_MOD * D), F32),
        grid=(depth, N_MOD * D // TN_MOD),
        in_specs=[
            pl.BlockSpec((MOD_ROWS, D), lambda l, j: (0, 0)),
            pl.BlockSpec((None, D, TN_MOD), lambda l, j: (l, 0, j)),
            pl.BlockSpec((None, 1, TN_MOD), lambda l, j: (l, 0, j)),
        ],
        out_specs=pl.BlockSpec((None, MOD_ROWS, TN_MOD), lambda l, j: (l, 0, j)),
        compiler_params=_params(2),
        name="modulation",
    )(c_all, w_mod, b_mod.reshape(depth, 1, N_MOD * D))


def _inproj_kernel(xl_ref, xc_ref, g_ref, sh_ref, sc_ref, w_ref, wg_ref, p_ref, gpre_ref, h_scr):
    i = pl.program_id(0)
    j = pl.program_id(1)

    @pl.when(j == 0)
    def _():
        x = jnp.where(i < N_LAT // TM_PROJ, xl_ref[...], xc_ref[...])
        h_scr[...] = _adaln(x, g_ref[...], sh_ref[...], sc_ref[...]).astype(BF16)
        gpre_ref[...] = jnp.dot(h_scr[...], wg_ref[...], preferred_element_type=F32)

    p_ref[...] = jnp.dot(h_scr[...], w_ref[...], preferred_element_type=F32).astype(BF16)


def _two_source_specs(tm):
    n_lat = N_LAT // tm
    return [
        pl.BlockSpec((tm, D), lambda i, *_: (jnp.minimum(i, n_lat - 1), 0)),
        pl.BlockSpec((tm, D), lambda i, *_: (jnp.maximum(i - n_lat, 0), 0)),
    ]


def _mlstm_inproj(xl, xc, gain, mod, w_main, w_gate):
    tm, tn = TM_PROJ, TN_INPROJ
    vec = pl.BlockSpec((1, D), lambda i, j: (0, 0))
    return pl.pallas_call(
        _inproj_kernel,
        out_shape=(jax.ShapeDtypeStruct((N_ALL, M_MAIN), BF16),
                   jax.ShapeDtypeStruct((N_ALL, GATE_PAD), F32)),
        grid=(N_ALL // tm, M_MAIN // tn),
        in_specs=_two_source_specs(tm) + [
            vec, _mod_spec(tm, 0), _mod_spec(tm, 1),
            pl.BlockSpec((D, tn), lambda i, j: (0, j)),
            pl.BlockSpec((D, GATE_PAD), lambda i, j: (0, 0)),
        ],
        out_specs=(pl.BlockSpec((tm, tn), lambda i, j: (i, j)),
                   pl.BlockSpec((tm, GATE_PAD), lambda i, j: (i, 0))),
        scratch_shapes=[pltpu.VMEM((tm, D), BF16)],
        compiler_params=_params(2),
        name="mlstm_inproj",
    )(xl, xc, gain, mod, mod, w_main, w_gate)


def _gateprep_kernel(gpre_ref, b_ref, gc_ref, gr_ref):
    L = gpre_ref.shape[0]
    z = gpre_ref[...] + b_ref[...]
    a = SOFTCAP * jnp.tanh(z * (1.0 / SOFTCAP))
    logsig = jnp.minimum(a, 0.0) - jnp.log(1.0 + jnp.exp(-jnp.abs(a)))
    row = lax.broadcasted_iota(jnp.int32, (L, GATE_PAD), 0)
    lane = lax.broadcasted_iota(jnp.int32, (L, GATE_PAD), 1)
    pre = logsig
    suf = logsig
    s = 1
    while s < L:
        pre = pre + jnp.where(row >= s, pltpu.roll(pre, s, 0), 0.0)
        suf = suf + jnp.where(row < L - s, pltpu.roll(suf, L - s, 0), 0.0)
        s *= 2
    out = jnp.where((lane >= MH) & (lane < 2 * MH), pre,
                    jnp.where((lane >= 3 * MH) & (lane < 4 * MH), suf, a))
    gc_ref[...] = out
    gr_ref[...] = out.T[:M_GATES, :]


def _gateprep(gpre, bias):
    L = L_SCAN
    return pl.pallas_call(
        _gateprep_kernel,
        out_shape=(jax.ShapeDtypeStruct((N_ALL, GATE_PAD), F32),
                   jax.ShapeDtypeStruct((M_GATES, N_ALL), F32)),
        grid=(N_ALL // L,),
        in_specs=[pl.BlockSpec((L, GATE_PAD), lambda i: (i, 0)),
                  pl.BlockSpec((1, GATE_PAD), lambda i: (0, 0))],
        out_specs=(pl.BlockSpec((L, GATE_PAD), lambda i: (i, 0)),
                   pl.BlockSpec((M_GATES, L), lambda i: (0, i))),
        compiler_params=_params(1),
        name="mlstm_gateprep",
    )(gpre, bias)


def _scan_unit(q, k, v, b_col, i_col, b_row, i_row, b_end, mask, c_ref, n_ref, m_ref, h_out):
    m = m_ref[0:1, 0:1]
    n = n_ref[0:1, :]
    C = c_ref[...]
    log_d = jnp.where(mask, b_col + (i_row - b_row), -jnp.inf)
    log_inter = b_col + m
    m_t = jnp.maximum(log_inter, jnp.max(log_d, axis=-1, keepdims=True))
    m_ts = m_t - M_SCALE_LOG
    dmat = jnp.exp(log_d - m_ts)
    inter = jnp.exp(log_inter - m_ts)
    sm = lax.dot_general(q, k, _NT, preferred_element_type=F32) * dmat
    q_c = jnp.dot(q, C.astype(BF16), preferred_element_type=F32)
    num = jnp.dot(sm.astype(BF16), v, preferred_element_type=F32) + inter * q_c
    q_n = jnp.sum(q.astype(F32) * n, axis=-1, keepdims=True)
    den = jnp.sum(sm, axis=-1, keepdims=True) + inter * q_n
    h_out[...] = num * (1.0 / jnp.maximum(jnp.abs(den), jnp.exp(-m_t)))

    log_w = b_end - b_col + i_col
    m_new = jnp.maximum(b_end + m, jnp.max(log_w, axis=0, keepdims=True))
    w = jnp.exp(log_w - m_new)
    decay = jnp.exp(b_end + m - m_new)
    wv = (w * v.astype(F32)).astype(BF16)
    c_ref[...] = decay * C + lax.dot_general(k, wv, _TN, preferred_element_type=F32)
    n_ref[0:1, :] = decay * n + jnp.sum(w * k.astype(F32), axis=0, keepdims=True)
    m_ref[...] = jnp.broadcast_to(m_new, m_ref.shape)


def _scan_kernel(qf, kf, vf, gcf, grf, qb, kb, vb, gcb, grb, hf_ref, hb_ref, c_scr, n_scr, m_scr):
    L = qf.shape[0]

    @pl.when(pl.program_id(1) == 0)
    def _():
        c_scr[...] = jnp.zeros_like(c_scr)
        n_scr[...] = jnp.zeros_like(n_scr)
        m_scr[...] = jnp.zeros_like(m_scr)

    row = lax.broadcasted_iota(jnp.int32, (L, L), 0)
    col = lax.broadcasted_iota(jnp.int32, (L, L), 1)
    directions = (
        (qf, kf, vf, gcf, grf, hf_ref, col <= row, 0, L - 1),
        (qb, kb, vb, gcb, grb, hb_ref, col >= row, 2 * MH, 0),
    )
    for d, (q_ref, k_ref, v_ref, gc_ref, gr_ref, h_ref, mask, goff, end) in enumerate(directions):
        for h in range(MH):
            ig, bc = goff + h, goff + MH + h
            b_row = gr_ref[bc:bc + 1, :]
            _scan_unit(
                q_ref[:, h * MDK:(h + 1) * MDK], k_ref[:, h * MDK:(h + 1) * MDK],
                v_ref[:, h * MDV:(h + 1) * MDV],
                gc_ref[:, bc:bc + 1], gc_ref[:, ig:ig + 1], b_row, gr_ref[ig:ig + 1, :],
                b_row[:, end:end + 1], mask,
                c_scr.at[d * MH + h], n_scr.at[d * MH + h], m_scr.at[d * MH + h],
                h_ref.at[:, h * MDV:(h + 1) * MDV])


def _mlstm_scan(p, gc, gr):
    L = L_SCAN
    n_ctx_chunks = CTX // L
    n_lat_chunks = SEQ // L
    steps = n_ctx_chunks + n_lat_chunks
    ctx0 = N_LAT // L

    def fwd(b, s):
        return jnp.where(s < n_ctx_chunks, ctx0 + b * n_ctx_chunks + s,
                         b * n_lat_chunks + s - n_ctx_chunks)

    def bwd(b, s):
        return jnp.where(s < n_ctx_chunks, ctx0 + b * n_ctx_chunks + (n_ctx_chunks - 1 - s),
                         b * n_lat_chunks + (n_lat_chunks - 1 - (s - n_ctx_chunks)))

    def specs(idx):
        return [
            pl.BlockSpec((L, MQK), lambda b, s: (idx(b, s), 0)),
            pl.BlockSpec((L, MQK), lambda b, s: (idx(b, s), 1)),
            pl.BlockSpec((L, MV), lambda b, s: (idx(b, s), 1)),
            pl.BlockSpec((L, GATE_PAD), lambda b, s: (idx(b, s), 0)),
            pl.BlockSpec((M_GATES, L), lambda b, s: (0, idx(b, s))),
        ]

    return pl.pallas_call(
        _scan_kernel,
        out_shape=(jax.ShapeDtypeStruct((N_ALL, MV), F32),
                   jax.ShapeDtypeStruct((N_ALL, MV), F32)),
        grid=(BATCH, steps),
        in_specs=specs(fwd) + specs(bwd),
        out_specs=(pl.BlockSpec((L, MV), lambda b, s: (fwd(b, s), 0)),
                   pl.BlockSpec((L, MV), lambda b, s: (bwd(b, s), 0))),
        scratch_shapes=[pltpu.VMEM((2 * MH, MDK, MDV), F32),
                        pltpu.VMEM((2 * MH, 8, MDK), F32),
                        pltpu.VMEM((2 * MH, 8, 128), F32)],
        compiler_params=_params(2),
        name="mlstm_scan",
    )(p, p, p, gc, gr, p, p, p, gc, gr)


def _mlstm_out_kernel(hf_ref, hb_ref, o_ref, xl_ref, xc_ref, gh_ref, gp_ref, gt_ref, w_ref,
                      out_ref, a_scr):
    i = pl.program_id(0)
    for h in range(MH):
        sl = slice(h * MDV, (h + 1) * MDV)
        hh = hf_ref[:, sl] + hb_ref[:, sl]
        r = lax.rsqrt(jnp.mean(hh * hh, axis=-1, keepdims=True) + EPS)
        hn = (hh * r) * gh_ref[:, sl]
        a_scr[:, sl] = (_sigmoid(o_ref[:, sl].astype(F32)) * hn).astype(BF16)
    y = jnp.dot(a_scr[...], w_ref[...], preferred_element_type=F32)
    x = jnp.where(i < N_LAT // TM_OUT, xl_ref[...], xc_ref[...])
    out_ref[...] = _post_norm_residual(x, y, gp_ref[...], gt_ref[...])


def _mlstm_out(hf, hb, p, xl, xc, g_head, g_post, mod, w_out):
    tm = TM_OUT
    vec = pl.BlockSpec((1, D), lambda i: (0, 0))
    return pl.pallas_call(
        _mlstm_out_kernel,
        out_shape=jax.ShapeDtypeStruct((N_ALL, D), F32),
        grid=(N_ALL // tm,),
        in_specs=[
            pl.BlockSpec((tm, MV), lambda i: (i, 0)),
            pl.BlockSpec((tm, MV), lambda i: (i, 0)),
            pl.BlockSpec((tm, MV), lambda i: (i, 2)),
        ] + _two_source_specs(tm) + [
            vec, vec,
            pl.BlockSpec((None, 1, D), lambda i: (_mod_row(i, tm), 0, 2)),
            pl.BlockSpec((MV, D), lambda i: (0, 0)),
        ],
        out_specs=pl.BlockSpec((tm, D), lambda i: (i, 0)),
        scratch_shapes=[pltpu.VMEM((tm, MV), BF16)],
        compiler_params=_params(1),
        name="mlstm_out",
    )(hf, hb, p, xl, xc, g_head, g_post, mod, w_out)


def _ffn_kernel(x_ref, g_ref, sh_ref, sc_ref, gt_ref, gp_ref, wg_ref, wu_ref, wo_ref, out_ref,
                h_scr, acc_scr):
    f = pl.program_id(1)

    @pl.when(f == 0)
    def _():
        h_scr[...] = _adaln(x_ref[...], g_ref[...], sh_ref[...], sc_ref[...]).astype(BF16)
        acc_scr[...] = jnp.zeros_like(acc_scr)

    h = h_scr[...]
    gate = jnp.dot(h, wg_ref[...], preferred_element_type=F32)
    up = jnp.dot(h, wu_ref[...], preferred_element_type=F32)
    act = ((gate * _sigmoid(gate)) * up).astype(BF16)
    acc_scr[...] += jnp.dot(act, wo_ref[...], preferred_element_type=F32)

    @pl.when(f == pl.num_programs(1) - 1)
    def _():
        out_ref[...] = _post_norm_residual(x_ref[...], acc_scr[...], gp_ref[...], gt_ref[...])


def _ffn(x_rows, g_pre, g_post, mod, w_in, w_out):
    n_rows = x_rows.shape[0]
    tm, tf = TM_FFN, TF_FFN
    nf = FFN // tf
    vec = pl.BlockSpec((1, D), lambda i, f: (0, 0))
    return pl.pallas_call(
        _ffn_kernel,
        out_shape=jax.ShapeDtypeStruct((n_rows, D), F32),
        grid=(n_rows // tm, nf),
        in_specs=[
            pl.BlockSpec((tm, D), lambda i, f: (i, 0)),
            vec, _mod_spec(tm, 3), _mod_spec(tm, 4), _mod_spec(tm, 5), vec,
            pl.BlockSpec((D, tf), lambda i, f: (0, f)),
            pl.BlockSpec((D, tf), lambda i, f: (0, nf + f)),
            pl.BlockSpec((tf, D), lambda i, f: (f, 0)),
        ],
        out_specs=pl.BlockSpec((tm, D), lambda i, f: (i, 0)),
        scratch_shapes=[pltpu.VMEM((tm, D), BF16), pltpu.VMEM((tm, D), F32)],
        compiler_params=_params(2),
        name="ffn",
    )(x_rows, g_pre, mod, mod, mod, g_post, w_in, w_in, w_out)


def _norm_rope(y, gain, cos, sin, scale):
    even = (lax.broadcasted_iota(jnp.int32, (y.shape[0], ADH), 1) & 1) == 0
    outs = []
    for h in range(y.shape[1] // ADH):
        yy = y[:, h * ADH:(h + 1) * ADH]
        r = lax.rsqrt(jnp.mean(yy * yy, axis=-1, keepdims=True) + EPS)
        yn = (yy * r) * gain
        partner = jnp.where(even, pltpu.roll(yn, ADH - 1, 1), pltpu.roll(yn, 1, 1))
        outs.append(((yn * cos + partner * sin) * scale).astype(BF16))
    return jnp.concatenate(outs, axis=1)


def _qkv_kernel(x_ref, g_ref, sh_ref, sc_ref, gq_ref, gk_ref, cos_ref, sin_ref, w_ref,
                q_ref, k_ref, vt_ref, h_scr):
    j = pl.program_id(1)
    n_q = AH * ADH // TN_QKV
    kv_cols = AKV * ADH

    @pl.when(j == 0)
    def _():
        h_scr[...] = _adaln(x_ref[...], g_ref[...], sh_ref[...], sc_ref[...]).astype(BF16)

    y = jnp.dot(h_scr[...], w_ref[...], preferred_element_type=F32)

    @pl.when(j < n_q)
    def _():
        q_ref[...] = _norm_rope(y, gq_ref[...], cos_ref[...], sin_ref[...], A_SCALE)

    @pl.when(j == n_q)
    def _():
        k_ref[...] = _norm_rope(y[:, :kv_cols], gk_ref[...], cos_ref[...], sin_ref[...], 1.0)
        vt = y[:, kv_cols:].T
        ones = jnp.ones((VT_ONES, vt.shape[1]), BF16)
        for h in range(AKV):
            vt_ref[h * VT_ROWS:h * VT_ROWS + ADH, :] = vt[h * ADH:(h + 1) * ADH, :].astype(BF16)
            vt_ref[h * VT_ROWS + ADH:(h + 1) * VT_ROWS, :] = ones


def _attn_qkv(x_all, gain, mod, g_q, g_k, cos_t, sin_t, w_qkv):
    tm, tn = TM_PROJ, TN_QKV
    n_q = AH * ADH // tn
    n_lat = N_LAT // tm
    per_seq = SEQ // tm
    kv_cols = AKV * ADH
    assert tn == 2 * kv_cols
    vec = pl.BlockSpec((1, D), lambda i, j: (0, 0))
    hvec = pl.BlockSpec((1, ADH), lambda i, j: (0, 0))
    rope = pl.BlockSpec((tm, ADH), lambda i, j: (jnp.where(i < n_lat, i % per_seq, per_seq), 0))
    return pl.pallas_call(
        _qkv_kernel,
        out_shape=(jax.ShapeDtypeStruct((N_ALL, AH * ADH), BF16),
                   jax.ShapeDtypeStruct((N_ALL, kv_cols), BF16),
                   jax.ShapeDtypeStruct((AKV * VT_ROWS, N_ALL), BF16)),
        grid=(N_ALL // tm, n_q + 1),
        in_specs=[
            pl.BlockSpec((tm, D), lambda i, j: (i, 0)),
            vec, _mod_spec(tm, 0), _mod_spec(tm, 1), hvec, hvec, rope, rope,
            pl.BlockSpec((D, tn), lambda i, j: (0, j)),
        ],
        out_specs=(pl.BlockSpec((tm, tn), lambda i, j: (i, jnp.minimum(j, n_q - 1))),
                   pl.BlockSpec((tm, kv_cols), lambda i, j: (i, 0)),
                   pl.BlockSpec((AKV * VT_ROWS, tm), lambda i, j: (0, i))),
        scratch_shapes=[pltpu.VMEM((tm, D), BF16)],
        compiler_params=_params(2),
        name="attn_qkv",
    )(x_all, gain, mod, mod, g_q, g_k, cos_t, sin_t, w_qkv)


def _attn_kernel(q_ref, kl_ref, kc_ref, vtl_ref, vtc_ref, o_ref):
    tq = q_ref.shape[0]
    nq = AG * tq
    q = jnp.concatenate([q_ref[:, g * ADH:(g + 1) * ADH] for g in range(AG)], axis=0)

    def block(k, vt, m, acc):
        st = lax.dot_general(k, q, _NT, preferred_element_type=F32)
        m_new = jnp.maximum(m, jnp.max(st, axis=0, keepdims=True))
        p = jnp.exp(st - m_new).astype(BF16)
        acc = jnp.exp(m - m_new) * acc + jnp.dot(vt, p, preferred_element_type=F32)
        return m_new, acc

    m = jnp.full((1, nq), -jnp.inf, F32)
    acc = jnp.zeros((VT_ROWS, nq), F32)
    for c in range(SEQ // TK_ATT):
        m, acc = block(kl_ref[c * TK_ATT:(c + 1) * TK_ATT, :],
                       vtl_ref[:, c * TK_ATT:(c + 1) * TK_ATT], m, acc)
    m, acc = block(kc_ref[...], vtc_ref[...], m, acc)
    out = (acc[:ADH, :] * (1.0 / acc[ADH:ADH + 1, :])).T
    for g in range(AG):
        o_ref[:, g * ADH:(g + 1) * ADH] = out[g * tq:(g + 1) * tq, :].astype(BF16)


def _attention(q, k, vt):
    tq = TQ_ATT
    per_seq = SEQ // tq
    return pl.pallas_call(
        _attn_kernel,
        out_shape=jax.ShapeDtypeStruct((N_LAT, AH * ADH), BF16),
        grid=(BATCH, AKV, per_seq),
        in_specs=[
            pl.BlockSpec((tq, AG * ADH), lambda b, h, i: (b * per_seq + i, h)),
            pl.BlockSpec((SEQ, ADH), lambda b, h, i: (b, h)),
            pl.BlockSpec((CTX, ADH), lambda b, h, i: (N_LAT // CTX + b, h)),
            pl.BlockSpec((VT_ROWS, SEQ), lambda b, h, i: (h, b)),
            pl.BlockSpec((VT_ROWS, CTX), lambda b, h, i: (h, N_LAT // CTX + b)),
        ],
        out_specs=pl.BlockSpec((tq, AG * ADH), lambda b, h, i: (b * per_seq + i, h)),
        compiler_params=_params(3),
        name="attention",
    )(q, k, k, vt, vt)


def _attn_out_kernel(o_ref, x_ref, gp_ref, gt_ref, w_ref, out_ref):
    y = jnp.dot(o_ref[...], w_ref[...], preferred_element_type=F32)
    out_ref[...] = _post_norm_residual(x_ref[...], y, gp_ref[...], gt_ref[...])


def _attn_out(o, x_all, g_post, mod, w_out):
    tm = TM_OUT
    return pl.pallas_call(
        _attn_out_kernel,
        out_shape=jax.ShapeDtypeStruct((N_LAT, D), F32),
        grid=(N_LAT // tm,),
        in_specs=[
            pl.BlockSpec((tm, D), lambda i: (i, 0)),
            pl.BlockSpec((tm, D), lambda i: (i, 0)),
            pl.BlockSpec((1, D), lambda i: (0, 0)),
            pl.BlockSpec((None, 1, D), lambda i: (_mod_row(i, tm), 0, 2)),
            pl.BlockSpec((D, D), lambda i: (0, 0)),
        ],
        out_specs=pl.BlockSpec((tm, D), lambda i: (i, 0)),
        compiler_params=_params(1),
        name="attn_out",
    )(o, x_all, g_post, mod, w_out)


def _rope_tables():
    rows = SEQ // GRID_W
    t_row = jnp.repeat(jnp.arange(rows, dtype=F32), GRID_W)
    t_col = jnp.tile(jnp.arange(GRID_W, dtype=F32), rows)
    per_axis = ADH // 2
    inv = ROPE_THETA ** (-jnp.arange(0, per_axis, 2, dtype=F32) / per_axis)
    ang = jnp.concatenate([t_row[:, None] * inv, t_col[:, None] * inv], axis=-1)
    cos = jnp.repeat(jnp.cos(ang), 2, axis=-1)
    sin = jnp.repeat(jnp.sin(ang), 2, axis=-1) * jnp.tile(jnp.array([-1.0, 1.0], F32), per_axis)
    pad = TM_PROJ
    cos = jnp.concatenate([cos, jnp.ones((pad, ADH), F32)], axis=0)
    sin = jnp.concatenate([sin, jnp.zeros((pad, ADH), F32)], axis=0)
    return cos, sin


def kernel(x, c, ctx, c_ctx, w_mod, b_mod, g_mix_pre, g_mix_post, g_ffn_pre, g_ffn_post,
           w_mlstm_in, b_mlstm_gate, g_mlstm_head, w_mlstm_out,
           w_attn_qkv, g_attn_q, g_attn_k, w_attn_out, w_ffn_in, w_ffn_out):
    assert x.shape == (BATCH, SEQ, D) and ctx.shape == (BATCH, CTX, D)
    xl = x.reshape(N_LAT, D)
    xc = ctx.reshape(N_CTX, D)
    row = lambda a: a.reshape(1, -1)

    c_all = jnp.concatenate([c, c_ctx[None, :], jnp.zeros((MOD_ROWS - BATCH - 1, D), F32)], axis=0)
    mod = _modulation(c_all, w_mod, b_mod).reshape(2, MOD_ROWS, 1, N_MOD * D)

    w_in = w_mlstm_in[0]
    w_main = w_in[:, :M_MAIN].astype(BF16)
    w_gate = jnp.pad(w_in[:, M_MAIN:], ((0, 0), (0, GATE_PAD - M_GATES))).astype(BF16)
    b_gate = jnp.pad(b_mlstm_gate[0], (0, GATE_PAD - M_GATES)).reshape(1, GATE_PAD)
    p, gpre = _mlstm_inproj(xl, xc, row(g_mix_pre[0]), mod[0], w_main, w_gate)
    gc, gr = _gateprep(gpre, b_gate)
    hf, hb = _mlstm_scan(p, gc, gr)
    x_all = _mlstm_out(hf, hb, p, xl, xc, row(g_mlstm_head[0]), row(g_mix_post[0]), mod[0],
                       w_mlstm_out[0].astype(BF16))
    x_all = _ffn(x_all, row(g_ffn_pre[0]), row(g_ffn_post[0]), mod[0],
                 w_ffn_in[0].astype(BF16), w_ffn_out[0].astype(BF16))

    cos_t, sin_t = _rope_tables()
    q, k, vt = _attn_qkv(x_all, row(g_mix_pre[1]), mod[1], row(g_attn_q[0]), row(g_attn_k[0]),
                         cos_t, sin_t, w_attn_qkv[0].astype(BF16))
    o = _attention(q, k, vt)
    x_lat = _attn_out(o, x_all, row(g_mix_post[1]), mod[1], w_attn_out[0].astype(BF16))
    x_lat = _ffn(x_lat, row(g_ffn_pre[1]), row(g_ffn_post[1]), mod[1],
                 w_ffn_in[1].astype(BF16), w_ffn_out[1].astype(BF16))
    return x_lat.reshape(BATCH, SEQ, D)
```

```python
import functools
import math

import jax
import jax.numpy as jnp
from jax import lax
from jax.experimental import pallas as pl
from jax.experimental.pallas import tpu as pltpu

F32 = jnp.float32
BF16 = jnp.bfloat16

D = 2048
BATCH = 4
SEQ = 4096
CTX = 256
N_LAT = BATCH * SEQ
N_CTX = BATCH * CTX
N_ALL = N_LAT + N_CTX
N_MOD = 6
EPS = 1e-6
MOD_ROWS = 8

MH = 8
MDK = 128
MDV = 256
MQK = MH * MDK
MV = MH * MDV
M_MAIN = 2 * MQK + 2 * MV
M_GATES = 4 * MH
GATE_PAD = 128
SOFTCAP = 15.0
M_SCALE_LOG = -0.5 * math.log(MDK)

AH = 16
AKV = 4
ADH = 128
AG = AH // AKV
A_SCALE = ADH ** -0.5
LOG2E = math.log2(math.e)
VT_ONES = 16
VT_ROWS = ADH + VT_ONES
GRID_W = 64
ROPE_THETA = 10000.0

FFN = 5632

VMEM_LIMIT = 56 * 1024 * 1024
TM_PROJ = 512
TN_INPROJ = 2048
TN_QKV = 1024
TM_OUT = 256
TM_ATT_OUT = 2 * TM_OUT
TM_FFN = 512
TF_FFN = 512
TN_MOD = 1024
L_SCAN = 256
TQ_ATT = 256
TK_ATT = 1024

_NT = (((1,), (1,)), ((), ()))
_TN = (((0,), (0,)), ((), ()))


def _params(n_axes):
    return pltpu.CompilerParams(
        dimension_semantics=("arbitrary",) * n_axes, vmem_limit_bytes=VMEM_LIMIT)


def _mod_row(i, tm):
    return jnp.where(i < N_LAT // tm, i // (SEQ // tm), BATCH)


def _mod_spec(tm, chunk):
    return pl.BlockSpec((None, 1, D), lambda i, j: (_mod_row(i, tm), 0, chunk))


def _adaln(x, gain, shift, scale):
    r = lax.rsqrt(jnp.mean(x * x, axis=-1, keepdims=True) + EPS)
    return (x * r) * (gain * (1.0 + scale)) + shift


def _post_norm_residual(x, y, gain, gate):
    r = lax.rsqrt(jnp.mean(y * y, axis=-1, keepdims=True) + EPS)
    return x + gate * ((y * r) * gain)


def _sigmoid(z):
    return 1.0 / (1.0 + jnp.exp(-z))


def _mod_kernel(c_ref, w_ref, b_ref, o_ref):
    c = c_ref[...]
    cond = (c * _sigmoid(c)).astype(BF16)
    o_ref[...] = jnp.dot(cond, w_ref[...].astype(BF16), preferred_element_type=F32) + b_ref[...]


def _modulation(c_all, w_mod, b_mod):
    depth = w_mod.shape[0]
    return pl.pallas_call(
        _mod_kernel,
        out_shape=jax.ShapeDtypeStruct((depth, MOD_ROWS, N_MOD * D), F32),
        grid=(depth, N_MOD * D // TN_MOD),
        in_specs=[
            pl.BlockSpec((MOD_ROWS, D), lambda l, j: (0, 0)),
            pl.BlockSpec((None, D, TN_MOD), lambda l, j: (l, 0, j)),
            pl.BlockSpec((None, 1, TN_MOD), lambda l, j: (l, 0, j)),
        ],
        out_specs=pl.BlockSpec((None, MOD_ROWS, TN_MOD), lambda l, j: (l, 0, j)),
        compiler_params=_params(2),
        name="modulation",
    )(c_all, w_mod, b_mod.reshape(depth, 1, N_MOD * D))


def _inproj_kernel(xl_ref, xc_ref, g_ref, sh_ref, sc_ref, w_ref, wg_ref, p_ref, gpre_ref, h_scr):
    i = pl.program_id(0)
    j = pl.program_id(1)

    @pl.when(j == 0)
    def _():
        x = jnp.where(i < N_LAT // TM_PROJ, xl_ref[...], xc_ref[...])
        h_scr[...] = _adaln(x, g_ref[...], sh_ref[...], sc_ref[...]).astype(BF16)
        gpre_ref[...] = jnp.dot(h_scr[...], wg_ref[...], preferred_element_type=F32)

    p_ref[...] = jnp.dot(h_scr[...], w_ref[...], preferred_element_type=F32).astype(BF16)


def _two_source_specs(tm):
    n_lat = N_LAT // tm
    return [
        pl.BlockSpec((tm, D), lambda i, *_: (jnp.minimum(i, n_lat - 1), 0)),
        pl.BlockSpec((tm, D), lambda i, *_: (jnp.maximum(i - n_lat, 0), 0)),
    ]


def _mlstm_inproj(xl, xc, gain, mod, w_main, w_gate):
    tm, tn = TM_PROJ, TN_INPROJ
    vec = pl.BlockSpec((1, D), lambda i, j: (0, 0))
    return pl.pallas_call(
        _inproj_kernel,
        out_shape=(jax.ShapeDtypeStruct((N_ALL, M_MAIN), BF16),
                   jax.ShapeDtypeStruct((N_ALL, GATE_PAD), F32)),
        grid=(N_ALL // tm, M_MAIN // tn),
        in_specs=_two_source_specs(tm) + [
            vec, _mod_spec(tm, 0), _mod_spec(tm, 1),
            pl.BlockSpec((D, tn), lambda i, j: (0, j)),
            pl.BlockSpec((D, GATE_PAD), lambda i, j: (0, 0)),
        ],
        out_specs=(pl.BlockSpec((tm, tn), lambda i, j: (i, j)),
                   pl.BlockSpec((tm, GATE_PAD), lambda i, j: (i, 0))),
        scratch_shapes=[pltpu.VMEM((tm, D), BF16)],
        compiler_params=_params(2),
        name="mlstm_inproj",
    )(xl, xc, gain, mod, mod, w_main, w_gate)


def _gateprep_kernel(gpre_ref, b_ref, gc_ref, gr_ref):
    L = gpre_ref.shape[0]
    z = gpre_ref[...] + b_ref[...]
    a = SOFTCAP * jnp.tanh(z * (1.0 / SOFTCAP))
    logsig = jnp.minimum(a, 0.0) - jnp.log(1.0 + jnp.exp(-jnp.abs(a)))
    row = lax.broadcasted_iota(jnp.int32, (L, GATE_PAD), 0)
    lane = lax.broadcasted_iota(jnp.int32, (L, GATE_PAD), 1)
    pre = logsig
    suf = logsig
    s = 1
    while s < L:
        pre = pre + jnp.where(row >= s, pltpu.roll(pre, s, 0), 0.0)
        suf = suf + jnp.where(row < L - s, pltpu.roll(suf, L - s, 0), 0.0)
        s *= 2
    out = jnp.where((lane >= MH) & (lane < 2 * MH), pre,
                    jnp.where((lane >= 3 * MH) & (lane < 4 * MH), suf, a))
    gc_ref[...] = out
    gr_ref[...] = out.T[:M_GATES, :]


def _gateprep(gpre, bias):
    L = L_SCAN
    return pl.pallas_call(
        _gateprep_kernel,
        out_shape=(jax.ShapeDtypeStruct((N_ALL, GATE_PAD), F32),
                   jax.ShapeDtypeStruct((M_GATES, N_ALL), F32)),
        grid=(N_ALL // L,),
        in_specs=[pl.BlockSpec((L, GATE_PAD), lambda i: (i, 0)),
                  pl.BlockSpec((1, GATE_PAD), lambda i: (0, 0))],
        out_specs=(pl.BlockSpec((L, GATE_PAD), lambda i: (i, 0)),
                   pl.BlockSpec((M_GATES, L), lambda i: (0, i))),
        compiler_params=_params(1),
        name="mlstm_gateprep",
    )(gpre, bias)


def _scan_unit(q, k, v, b_col, i_col, b_row, i_row, b_end, mask, c_ref, n_ref, m_ref, h_out):
    m = m_ref[0:1, 0:1]
    n = n_ref[0:1, :]
    C = c_ref[...]
    log_d = jnp.where(mask, b_col + (i_row - b_row), -jnp.inf)
    log_inter = b_col + m
    m_t = jnp.maximum(log_inter, jnp.max(log_d, axis=-1, keepdims=True))
    m_ts = m_t - M_SCALE_LOG
    dmat = jnp.exp(log_d - m_ts)
    inter = jnp.exp(log_inter - m_ts)
    sm = lax.dot_general(q, k, _NT, preferred_element_type=F32) * dmat
    q_c = jnp.dot(q, C.astype(BF16), preferred_element_type=F32)
    num = jnp.dot(sm.astype(BF16), v, preferred_element_type=F32) + inter * q_c
    q_n = jnp.sum(q.astype(F32) * n, axis=-1, keepdims=True)
    den = jnp.sum(sm, axis=-1, keepdims=True) + inter * q_n
    h_out[...] = num * (1.0 / jnp.maximum(jnp.abs(den), jnp.exp(-m_t)))

    log_w = b_end - b_col + i_col
    m_new = jnp.maximum(b_end + m, jnp.max(log_w, axis=0, keepdims=True))
    w = jnp.exp(log_w - m_new)
    decay = jnp.exp(b_end + m - m_new)
    wv = (w * v.astype(F32)).astype(BF16)
    c_ref[...] = decay * C + lax.dot_general(k, wv, _TN, preferred_element_type=F32)
    n_ref[0:1, :] = decay * n + jnp.sum(w * k.astype(F32), axis=0, keepdims=True)
    m_ref[...] = jnp.broadcast_to(m_new, m_ref.shape)


def _scan_kernel(qf, kf, vf, gcf, grf, qb, kb, vb, gcb, grb, hf_ref, hb_ref, c_scr, n_scr, m_scr):
    L = qf.shape[0]

    @pl.when(pl.program_id(1) == 0)
    def _():
        c_scr[...] = jnp.zeros_like(c_scr)
        n_scr[...] = jnp.zeros_like(n_scr)
        m_scr[...] = jnp.zeros_like(m_scr)

    row = lax.broadcasted_iota(jnp.int32, (L, L), 0)
    col = lax.broadcasted_iota(jnp.int32, (L, L), 1)
    directions = (
        (qf, kf, vf, gcf, grf, hf_ref, col <= row, 0, L - 1),
        (qb, kb, vb, gcb, grb, hb_ref, col >= row, 2 * MH, 0),
    )
    for d, (q_ref, k_ref, v_ref, gc_ref, gr_ref, h_ref, mask, goff, end) in enumerate(directions):
        for h in range(MH):
            ig, bc = goff + h, goff + MH + h
            b_row = gr_ref[bc:bc + 1, :]
            _scan_unit(
                q_ref[:, h * MDK:(h + 1) * MDK], k_ref[:, h * MDK:(h + 1) * MDK],
                v_ref[:, h * MDV:(h + 1) * MDV],
                gc_ref[:, bc:bc + 1], gc_ref[:, ig:ig + 1], b_row, gr_ref[ig:ig + 1, :],
                b_row[:, end:end + 1], mask,
                c_scr.at[d * MH + h], n_scr.at[d * MH + h], m_scr.at[d * MH + h],
                h_ref.at[:, h * MDV:(h + 1) * MDV])


def _mlstm_scan(p, gc, gr):
    L = L_SCAN
    n_ctx_chunks = CTX // L
    n_lat_chunks = SEQ // L
    steps = n_ctx_chunks + n_lat_chunks
    ctx0 = N_LAT // L

    def fwd(b, s):
        return jnp.where(s < n_ctx_chunks, ctx0 + b * n_ctx_chunks + s,
                         b * n_lat_chunks + s - n_ctx_chunks)

    def bwd(b, s):
        return jnp.where(s < n_ctx_chunks, ctx0 + b * n_ctx_chunks + (n_ctx_chunks - 1 - s),
                         b * n_lat_chunks + (n_lat_chunks - 1 - (s - n_ctx_chunks)))

    def specs(idx):
        return [
            pl.BlockSpec((L, MQK), lambda b, s: (idx(b, s), 0)),
            pl.BlockSpec((L, MQK), lambda b, s: (idx(b, s), 1)),
            pl.BlockSpec((L, MV), lambda b, s: (idx(b, s), 1)),
            pl.BlockSpec((L, GATE_PAD), lambda b, s: (idx(b, s), 0)),
            pl.BlockSpec((M_GATES, L), lambda b, s: (0, idx(b, s))),
        ]

    return pl.pallas_call(
        _scan_kernel,
        out_shape=(jax.ShapeDtypeStruct((N_ALL, MV), F32),
                   jax.ShapeDtypeStruct((N_ALL, MV), F32)),
        grid=(BATCH, steps),
        in_specs=specs(fwd) + specs(bwd),
        out_specs=(pl.BlockSpec((L, MV), lambda b, s: (fwd(b, s), 0)),
                   pl.BlockSpec((L, MV), lambda b, s: (bwd(b, s), 0))),
        scratch_shapes=[pltpu.VMEM((2 * MH, MDK, MDV), F32),
                        pltpu.VMEM((2 * MH, 8, MDK), F32),
                        pltpu.VMEM((2 * MH, 8, 128), F32)],
        compiler_params=_params(2),
        name="mlstm_scan",
    )(p, p, p, gc, gr, p, p, p, gc, gr)


def _mlstm_out_kernel(hf_ref, hb_ref, o_ref, xl_ref, xc_ref, gh_ref, gp_ref, gt_ref, w_ref,
                      out_ref, a_scr):
    i = pl.program_id(0)
    for h in range(MH):
        sl = slice(h * MDV, (h + 1) * MDV)
        hh = hf_ref[:, sl] + hb_ref[:, sl]
        r = lax.rsqrt(jnp.mean(hh * hh, axis=-1, keepdims=True) + EPS)
        hn = (hh * r) * gh_ref[:, sl]
        a_scr[:, sl] = (_sigmoid(o_ref[:, sl].astype(F32)) * hn).astype(BF16)
    y = jnp.dot(a_scr[...], w_ref[...], preferred_element_type=F32)
    x = jnp.where(i < N_LAT // TM_OUT, xl_ref[...], xc_ref[...])
    out_ref[...] = _post_norm_residual(x, y, gp_ref[...], gt_ref[...])


def _mlstm_out(hf, hb, p, xl, xc, g_head, g_post, mod, w_out):
    tm = TM_OUT
    vec = pl.BlockSpec((1, D), lambda i: (0, 0))
    return pl.pallas_call(
        _mlstm_out_kernel,
        out_shape=jax.ShapeDtypeStruct((N_ALL, D), F32),
        grid=(N_ALL // tm,),
        in_specs=[
            pl.BlockSpec((tm, MV), lambda i: (i, 0)),
            pl.BlockSpec((tm, MV), lambda i: (i, 0)),
            pl.BlockSpec((tm, MV), lambda i: (i, 2)),
        ] + _two_source_specs(tm) + [
            vec, vec,
            pl.BlockSpec((None, 1, D), lambda i: (_mod_row(i, tm), 0, 2)),
            pl.BlockSpec((MV, D), lambda i: (0, 0)),
        ],
        out_specs=pl.BlockSpec((tm, D), lambda i: (i, 0)),
        scratch_shapes=[pltpu.VMEM((tm, MV), BF16)],
        compiler_params=_params(1),
        name="mlstm_out",
    )(hf, hb, p, xl, xc, g_head, g_post, mod, w_out)


def _ffn_kernel(x_ref, g_ref, sh_ref, sc_ref, gt_ref, gp_ref, wg_ref, wu_ref, wo_ref, out_ref,
                h_scr, acc_scr):
    f = pl.program_id(1)

    @pl.when(f == 0)
    def _():
        h_scr[...] = _adaln(x_ref[...], g_ref[...], sh_ref[...], sc_ref[...]).astype(BF16)
        acc_scr[...] = jnp.zeros_like(acc_scr)

    h = h_scr[...]
    gate = jnp.dot(h, wg_ref[...], preferred_element_type=F32)
    up = jnp.dot(h, wu_ref[...], preferred_element_type=F32)
    act = ((gate * _sigmoid(gate)) * up).astype(BF16)
    acc_scr[...] += jnp.dot(act, wo_ref[...], preferred_element_type=F32)

    @pl.when(f == pl.num_programs(1) - 1)
    def _():
        out_ref[...] = _post_norm_residual(x_ref[...], acc_scr[...], gp_ref[...], gt_ref[...])


def _ffn(x_rows, g_pre, g_post, mod, w_in, w_out):
    n_rows = x_rows.shape[0]
    tm, tf = TM_FFN, TF_FFN
    nf = FFN // tf
    vec = pl.BlockSpec((1, D), lambda i, f: (0, 0))
    return pl.pallas_call(
        _ffn_kernel,
        out_shape=jax.ShapeDtypeStruct((n_rows, D), F32),
        grid=(n_rows // tm, nf),
        in_specs=[
            pl.BlockSpec((tm, D), lambda i, f: (i, 0)),
            vec, _mod_spec(tm, 3), _mod_spec(tm, 4), _mod_spec(tm, 5), vec,
            pl.BlockSpec((D, tf), lambda i, f: (0, f)),
            pl.BlockSpec((D, tf), lambda i, f: (0, nf + f)),
            pl.BlockSpec((tf, D), lambda i, f: (f, 0)),
        ],
        out_specs=pl.BlockSpec((tm, D), lambda i, f: (i, 0)),
        scratch_shapes=[pltpu.VMEM((tm, D), BF16), pltpu.VMEM((tm, D), F32)],
        compiler_params=_params(2),
        name="ffn",
    )(x_rows, g_pre, mod, mod, mod, g_post, w_in, w_in, w_out)


def _norm_rope_pair(h_bf16, w_ref, cols, a, b):
    y = jnp.dot(h_bf16, w_ref[:, cols], preferred_element_type=F32)
    outs = []
    for h in range(2):
        yy = y[:, h * ADH:(h + 1) * ADH]
        r = lax.rsqrt(jnp.mean(yy * yy, axis=-1, keepdims=True) + EPS)
        outs.append(((yy * a + pltpu.roll(yy, ADH // 2, 1) * b) * r).astype(BF16))
    return jnp.concatenate(outs, axis=1)


def _qkv_kernel(x_ref, g_ref, sh_ref, sc_ref, aq_ref, bq_ref, ak_ref, bk_ref, w_ref,
                q_ref, k_ref, vt_ref, h_scr):
    j = pl.program_id(1)
    n_q = AH * ADH // TN_QKV
    kv_cols = AKV * ADH
    pair = 2 * ADH

    @pl.when(j == 0)
    def _():
        h_scr[...] = _adaln(x_ref[...], g_ref[...], sh_ref[...], sc_ref[...]).astype(BF16)

    @pl.when(j < n_q)
    def _():
        h = h_scr[...]
        for c in range(0, TN_QKV, pair):
            q_ref[:, c:c + pair] = _norm_rope_pair(h, w_ref, slice(c, c + pair),
                                                   aq_ref[...], bq_ref[...])

    @pl.when(j == n_q)
    def _():
        h = h_scr[...]
        for c in range(0, kv_cols, pair):
            k_ref[:, c:c + pair] = _norm_rope_pair(h, w_ref, slice(c, c + pair),
                                                   ak_ref[...], bk_ref[...])
        v = jnp.dot(h, w_ref[:, kv_cols:], preferred_element_type=F32)
        vt = v.T
        ones = jnp.ones((VT_ONES, vt.shape[1]), BF16)
        for hd in range(AKV):
            vt_ref[hd * VT_ROWS:hd * VT_ROWS + ADH, :] = vt[hd * ADH:(hd + 1) * ADH, :].astype(BF16)
            vt_ref[hd * VT_ROWS + ADH:(hd + 1) * VT_ROWS, :] = ones


def _attn_qkv(x_all, gain, mod, a_q, b_q, a_k, b_k, w_qkv):
    tm, tn = TM_PROJ, TN_QKV
    n_q = AH * ADH // tn
    n_lat = N_LAT // tm
    per_seq = SEQ // tm
    kv_cols = AKV * ADH
    assert tn == 2 * kv_cols
    vec = pl.BlockSpec((1, D), lambda i, j: (0, 0))
    rope = pl.BlockSpec((tm, ADH), lambda i, j: (jnp.where(i < n_lat, i % per_seq, per_seq), 0))
    return pl.pallas_call(
        _qkv_kernel,
        out_shape=(jax.ShapeDtypeStruct((N_ALL, AH * ADH), BF16),
                   jax.ShapeDtypeStruct((N_ALL, kv_cols), BF16),
                   jax.ShapeDtypeStruct((AKV * VT_ROWS, N_ALL), BF16)),
        grid=(N_ALL // tm, n_q + 1),
        in_specs=[
            pl.BlockSpec((tm, D), lambda i, j: (i, 0)),
            vec, _mod_spec(tm, 0), _mod_spec(tm, 1), rope, rope, rope, rope,
            pl.BlockSpec((D, tn), lambda i, j: (0, j)),
        ],
        out_specs=(pl.BlockSpec((tm, tn), lambda i, j: (i, jnp.minimum(j, n_q - 1))),
                   pl.BlockSpec((tm, kv_cols), lambda i, j: (i, 0)),
                   pl.BlockSpec((AKV * VT_ROWS, tm), lambda i, j: (0, i))),
        scratch_shapes=[pltpu.VMEM((tm, D), BF16)],
        compiler_params=_params(2),
        name="attn_qkv",
    )(x_all, gain, mod, mod, a_q, b_q, a_k, b_k, w_qkv)


def _attn_kernel(q_ref, kl_ref, kc_ref, vtl_ref, vtc_ref, o_ref):
    tq = q_ref.shape[0]
    nq = AG * tq
    q = jnp.concatenate([q_ref[:, g * ADH:(g + 1) * ADH] for g in range(AG)], axis=0)

    def block(k, vt, m, acc):
        st = lax.dot_general(k, q, _NT, preferred_element_type=F32)
        m_new = jnp.maximum(m, jnp.max(st, axis=0, keepdims=True))
        p = jnp.exp2(st - m_new).astype(BF16)
        acc = jnp.exp2(m - m_new) * acc + jnp.dot(vt, p, preferred_element_type=F32)
        return m_new, acc

    m = jnp.full((1, nq), -jnp.inf, F32)
    acc = jnp.zeros((VT_ROWS, nq), F32)
    for c in range(SEQ // TK_ATT):
        m, acc = block(kl_ref[c * TK_ATT:(c + 1) * TK_ATT, :],
                       vtl_ref[:, c * TK_ATT:(c + 1) * TK_ATT], m, acc)
    m, acc = block(kc_ref[...], vtc_ref[...], m, acc)
    out = (acc[:ADH, :] * (1.0 / acc[ADH:ADH + 1, :])).T
    for g in range(AG):
        o_ref[:, g * ADH:(g + 1) * ADH] = out[g * tq:(g + 1) * tq, :].astype(BF16)


def _attention(q, k, vt):
    tq = TQ_ATT
    per_seq = SEQ // tq
    return pl.pallas_call(
        _attn_kernel,
        out_shape=jax.ShapeDtypeStruct((N_LAT, AH * ADH), BF16),
        grid=(BATCH, AKV, per_seq),
        in_specs=[
            pl.BlockSpec((tq, AG * ADH), lambda b, h, i: (b * per_seq + i, h)),
            pl.BlockSpec((SEQ, ADH), lambda b, h, i: (b, h)),
            pl.BlockSpec((CTX, ADH), lambda b, h, i: (N_LAT // CTX + b, h)),
            pl.BlockSpec((VT_ROWS, SEQ), lambda b, h, i: (h, b)),
            pl.BlockSpec((VT_ROWS, CTX), lambda b, h, i: (h, N_LAT // CTX + b)),
        ],
        out_specs=pl.BlockSpec((tq, AG * ADH), lambda b, h, i: (b * per_seq + i, h)),
        compiler_params=_params(3),
        name="attention",
    )(q, k, k, vt, vt)


def _attn_out_kernel(o_ref, x_ref, gp_ref, gt_ref, w_ref, out_ref):
    for r in range(0, TM_ATT_OUT, TM_OUT):
        rows = slice(r, r + TM_OUT)
        y = jnp.dot(o_ref[rows, :], w_ref[...], preferred_element_type=F32)
        out_ref[rows, :] = _post_norm_residual(x_ref[rows, :], y, gp_ref[...], gt_ref[...])


def _attn_out(o, x_all, g_post, mod, w_out):
    tm = TM_ATT_OUT
    return pl.pallas_call(
        _attn_out_kernel,
        out_shape=jax.ShapeDtypeStruct((N_LAT, D), F32),
        grid=(N_LAT // tm,),
        in_specs=[
            pl.BlockSpec((tm, D), lambda i: (i, 0)),
            pl.BlockSpec((tm, D), lambda i: (i, 0)),
            pl.BlockSpec((1, D), lambda i: (0, 0)),
            pl.BlockSpec((None, 1, D), lambda i: (_mod_row(i, tm), 0, 2)),
            pl.BlockSpec((D, D), lambda i: (0, 0)),
        ],
        out_specs=pl.BlockSpec((tm, D), lambda i: (i, 0)),
        compiler_params=_params(1),
        name="attn_out",
    )(o, x_all, g_post, mod, w_out)


def _half_split(a):
    lead = a.shape[:-1]
    n = a.shape[-1] // ADH
    a = a.reshape(*lead, n, ADH // 2, 2)
    return jnp.swapaxes(a, -1, -2).reshape(*lead, n * ADH)


def _rope_tables(gain, scale):
    rows = SEQ // GRID_W
    t_row = jnp.repeat(jnp.arange(rows, dtype=F32), GRID_W)
    t_col = jnp.tile(jnp.arange(GRID_W, dtype=F32), rows)
    per_axis = ADH // 2
    inv = ROPE_THETA ** (-jnp.arange(0, per_axis, 2, dtype=F32) / per_axis)
    ang = jnp.concatenate([t_row[:, None] * inv, t_col[:, None] * inv], axis=-1)
    ang = jnp.concatenate([ang, jnp.zeros((TM_PROJ, per_axis), F32)], axis=0)
    cos = jnp.concatenate([jnp.cos(ang), jnp.cos(ang)], axis=-1)
    sin = jnp.concatenate([-jnp.sin(ang), jnp.sin(ang)], axis=-1)
    g = _half_split(gain.astype(F32))
    return (g * scale) * cos, (jnp.roll(g, per_axis) * scale) * sin


def kernel(x, c, ctx, c_ctx, w_mod, b_mod, g_mix_pre, g_mix_post, g_ffn_pre, g_ffn_post,
           w_mlstm_in, b_mlstm_gate, g_mlstm_head, w_mlstm_out,
           w_attn_qkv, g_attn_q, g_attn_k, w_attn_out, w_ffn_in, w_ffn_out):
    assert x.shape == (BATCH, SEQ, D) and ctx.shape == (BATCH, CTX, D)
    xl = x.reshape(N_LAT, D)
    xc = ctx.reshape(N_CTX, D)
    row = lambda a: a.reshape(1, -1)

    c_all = jnp.concatenate([c, c_ctx[None, :], jnp.zeros((MOD_ROWS - BATCH - 1, D), F32)], axis=0)
    mod = _modulation(c_all, w_mod, b_mod).reshape(2, MOD_ROWS, 1, N_MOD * D)

    w_in = w_mlstm_in[0]
    w_main = w_in[:, :M_MAIN].astype(BF16)
    w_gate = jnp.pad(w_in[:, M_MAIN:], ((0, 0), (0, GATE_PAD - M_GATES))).astype(BF16)
    b_gate = jnp.pad(b_mlstm_gate[0], (0, GATE_PAD - M_GATES)).reshape(1, GATE_PAD)
    p, gpre = _mlstm_inproj(xl, xc, row(g_mix_pre[0]), mod[0], w_main, w_gate)
    gc, gr = _gateprep(gpre, b_gate)
    hf, hb = _mlstm_scan(p, gc, gr)
    x_all = _mlstm_out(hf, hb, p, xl, xc, row(g_mlstm_head[0]), row(g_mix_post[0]), mod[0],
                       w_mlstm_out[0].astype(BF16))
    x_all = _ffn(x_all, row(g_ffn_pre[0]), row(g_ffn_post[0]), mod[0],
                 w_ffn_in[0].astype(BF16), w_ffn_out[0].astype(BF16))

    n_qk = (AH + AKV) * ADH
    w_qkv = jnp.concatenate([_half_split(w_attn_qkv[0][:, :n_qk]), w_attn_qkv[0][:, n_qk:]],
                            axis=1).astype(BF16)
    a_q, b_q = _rope_tables(g_attn_q[0], A_SCALE * LOG2E)
    a_k, b_k = _rope_tables(g_attn_k[0], 1.0)
    q, k, vt = _attn_qkv(x_all, row(g_mix_pre[1]), mod[1], a_q, b_q, a_k, b_k, w_qkv)
    o = _attention(q, k, vt)
    x_lat = _attn_out(o, x_all, row(g_mix_post[1]), mod[1], w_attn_out[0].astype(BF16))
    x_lat = _ffn(x_lat, row(g_ffn_pre[1]), row(g_ffn_post[1]), mod[1],
                 w_ffn_in[1].astype(BF16), w_ffn_out[1].astype(BF16))
    return x_lat.reshape(BATCH, SEQ, D)
```

```python
import functools
import math

import jax
import jax.numpy as jnp
from jax import lax
from jax.experimental import pallas as pl
from jax.experimental.pallas import tpu as pltpu

F32 = jnp.float32
BF16 = jnp.bfloat16

D = 2048
BATCH = 4
SEQ = 4096
CTX = 256
N_LAT = BATCH * SEQ
N_CTX = BATCH * CTX
N_ALL = N_LAT + N_CTX
N_MOD = 6
EPS = 1e-6
MOD_ROWS = 8

MH = 8
MDK = 128
MDV = 256
MQK = MH * MDK
MV = MH * MDV
M_MAIN = 2 * MQK + 2 * MV
M_GATES = 4 * MH
GATE_PAD = 128
SOFTCAP = 15.0
M_SCALE_LOG2 = -0.5 * math.log2(MDK)
LANES = 128

AH = 16
AKV = 4
ADH = 128
AG = AH // AKV
A_SCALE = ADH ** -0.5
LOG2E = math.log2(math.e)
VT_ONES = 16
VT_ROWS = ADH + VT_ONES
GRID_W = 64
ROPE_THETA = 10000.0

FFN = 5632

VMEM_LIMIT = 56 * 1024 * 1024
TM_PROJ = 512
TN_INPROJ = 2048
TN_QKV = 1024
TM_OUT = 256
TM_ATT_OUT = 2 * TM_OUT
TM_FFN = 512
TF_FFN = 512
TN_MOD = 1024
L_SCAN = 256
TQ_ATT = 512
TK_ATT = 1024

_NT = (((1,), (1,)), ((), ()))
_TN = (((0,), (0,)), ((), ()))


def _params(n_axes):
    return pltpu.CompilerParams(
        dimension_semantics=("arbitrary",) * n_axes, vmem_limit_bytes=VMEM_LIMIT)


def _mod_row(i, tm):
    return jnp.where(i < N_LAT // tm, i // (SEQ // tm), BATCH)


def _mod_spec(tm, chunk):
    return pl.BlockSpec((None, 1, D), lambda i, j: (_mod_row(i, tm), 0, chunk))


def _adaln(x, gain, shift, scale):
    r = lax.rsqrt(jnp.mean(x * x, axis=-1, keepdims=True) + EPS)
    return (x * r) * (gain * (1.0 + scale)) + shift


def _post_norm_residual(x, y, gain, gate):
    r = lax.rsqrt(jnp.mean(y * y, axis=-1, keepdims=True) + EPS)
    return x + gate * ((y * r) * gain)


def _sigmoid(z):
    return 1.0 / (1.0 + jnp.exp(-z))


def _mod_kernel(c_ref, w_ref, b_ref, o_ref):
    c = c_ref[...]
    cond = (c * _sigmoid(c)).astype(BF16)
    o_ref[...] = jnp.dot(cond, w_ref[...].astype(BF16), preferred_element_type=F32) + b_ref[...]


def _modulation(c_all, w_mod, b_mod):
    depth = w_mod.shape[0]
    return pl.pallas_call(
        _mod_kernel,
        out_shape=jax.ShapeDtypeStruct((depth, MOD_ROWS, N_MOD * D), F32),
        grid=(depth, N_MOD * D // TN_MOD),
        in_specs=[
            pl.BlockSpec((MOD_ROWS, D), lambda l, j: (0, 0)),
            pl.BlockSpec((None, D, TN_MOD), lambda l, j: (l, 0, j)),
            pl.BlockSpec((None, 1, TN_MOD), lambda l, j: (l, 0, j)),
        ],
        out_specs=pl.BlockSpec((None, MOD_ROWS, TN_MOD), lambda l, j: (l, 0, j)),
        compiler_params=_params(2),
        name="modulation",
    )(c_all, w_mod, b_mod.reshape(depth, 1, N_MOD * D))


def _inproj_kernel(xl_ref, xc_ref, g_ref, sh_ref, sc_ref, w_ref, wg_ref, p_ref, gpre_ref, h_scr):
    i = pl.program_id(0)
    j = pl.program_id(1)

    @pl.when(j == 0)
    def _():
        x = jnp.where(i < N_LAT // TM_PROJ, xl_ref[...], xc_ref[...])
        h_scr[...] = _adaln(x, g_ref[...], sh_ref[...], sc_ref[...]).astype(BF16)
        gpre_ref[...] = jnp.dot(h_scr[...], wg_ref[...], preferred_element_type=F32)

    p_ref[...] = jnp.dot(h_scr[...], w_ref[...], preferred_element_type=F32).astype(BF16)


def _two_source_specs(tm):
    n_lat = N_LAT // tm
    return [
        pl.BlockSpec((tm, D), lambda i, *_: (jnp.minimum(i, n_lat - 1), 0)),
        pl.BlockSpec((tm, D), lambda i, *_: (jnp.maximum(i - n_lat, 0), 0)),
    ]


def _mlstm_inproj(xl, xc, gain, mod, w_main, w_gate):
    tm, tn = TM_PROJ, TN_INPROJ
    vec = pl.BlockSpec((1, D), lambda i, j: (0, 0))
    return pl.pallas_call(
        _inproj_kernel,
        out_shape=(jax.ShapeDtypeStruct((N_ALL, M_MAIN), BF16),
                   jax.ShapeDtypeStruct((N_ALL, GATE_PAD), F32)),
        grid=(N_ALL // tm, M_MAIN // tn),
        in_specs=_two_source_specs(tm) + [
            vec, _mod_spec(tm, 0), _mod_spec(tm, 1),
            pl.BlockSpec((D, tn), lambda i, j: (0, j)),
            pl.BlockSpec((D, GATE_PAD), lambda i, j: (0, 0)),
        ],
        out_specs=(pl.BlockSpec((tm, tn), lambda i, j: (i, j)),
                   pl.BlockSpec((tm, GATE_PAD), lambda i, j: (i, 0))),
        scratch_shapes=[pltpu.VMEM((tm, D), BF16)],
        compiler_params=_params(2),
        name="mlstm_inproj",
    )(xl, xc, gain, mod, mod, w_main, w_gate)


def _gateprep_kernel(gpre_ref, b_ref, gc_ref, gr_ref):
    L = gpre_ref.shape[0]
    z = gpre_ref[...] + b_ref[...]
    a = SOFTCAP * jnp.tanh(z * (1.0 / SOFTCAP))
    logsig = jnp.minimum(a, 0.0) - jnp.log(1.0 + jnp.exp(-jnp.abs(a)))
    row = lax.broadcasted_iota(jnp.int32, (L, GATE_PAD), 0)
    lane = lax.broadcasted_iota(jnp.int32, (L, GATE_PAD), 1)

    def scans(x, op, fill):
        pre, suf = x, x
        s = 1
        while s < L:
            pre = op(pre, jnp.where(row >= s, pltpu.roll(pre, s, 0), fill))
            suf = op(suf, jnp.where(row < L - s, pltpu.roll(suf, L - s, 0), fill))
            s *= 2
        return pre, suf

    fwd_lanes = lane < 2 * MH
    b_pre, b_suf = scans(logsig, jnp.add, 0.0)
    b = jnp.where(fwd_lanes, b_pre, b_suf)
    r = a - pltpu.roll(b, GATE_PAD - MH, 1)
    c_pre, c_suf = scans(r, jnp.maximum, -jnp.inf)
    cmax = jnp.where(fwd_lanes, c_pre, c_suf)
    is_gate_lane = (lane < MH) | ((lane >= 2 * MH) & (lane < 3 * MH))
    gc_ref[...] = jnp.where(is_gate_lane, cmax, b) * LOG2E
    gr_ref[...] = (r * LOG2E).T[:M_GATES, :]


def _gateprep(gpre, bias):
    L = L_SCAN
    return pl.pallas_call(
        _gateprep_kernel,
        out_shape=(jax.ShapeDtypeStruct((N_ALL, GATE_PAD), F32),
                   jax.ShapeDtypeStruct((M_GATES, N_ALL), F32)),
        grid=(N_ALL // L,),
        in_specs=[pl.BlockSpec((L, GATE_PAD), lambda i: (i, 0)),
                  pl.BlockSpec((1, GATE_PAD), lambda i: (0, 0))],
        out_specs=(pl.BlockSpec((L, GATE_PAD), lambda i: (i, 0)),
                   pl.BlockSpec((M_GATES, L), lambda i: (0, i))),
        compiler_params=_params(1),
        name="mlstm_gateprep",
    )(gpre, bias)


def _scan_unit(q, k, vx, r_row, cmax_col, b_col, mask, end, cx_ref, m_ref, h_out):
    L = q.shape[0]
    m = m_ref[0:1, 0:1]
    u = jnp.maximum(m, cmax_col)
    u_b = jnp.broadcast_to(u - M_SCALE_LOG2, (L, L))
    dmat = jnp.where(mask, jnp.exp2(r_row - u_b), 0.0)
    sm = (lax.dot_general(q, k, _NT, preferred_element_type=F32) * dmat).astype(BF16)
    cx = cx_ref[...]
    intra = jnp.dot(sm, vx, preferred_element_type=F32)
    inter = jnp.dot(q, cx.astype(BF16), preferred_element_type=F32)
    u_rep = u_b[:, :LANES]
    w_inter = jnp.exp2(m - u_rep)
    den = intra[:, MDV:] + w_inter * inter[:, MDV:]
    floor = jnp.exp2(-(jnp.broadcast_to(b_col, (L, LANES)) + u_rep + M_SCALE_LOG2))
    inv = 1.0 / jnp.maximum(jnp.abs(den), floor)
    for t in range(MDV // LANES):
        cols = slice(t * LANES, (t + 1) * LANES)
        h_out[:, cols] = (intra[:, cols] + w_inter * inter[:, cols]) * inv

    u_end = u[end:end + 1, :]
    w_row = jnp.exp2(r_row - u_end)
    kw = (k.astype(F32).T * w_row).astype(BF16)
    cx_ref[...] = jnp.exp2(m - u_end) * cx + jnp.dot(kw, vx, preferred_element_type=F32)
    m_ref[...] = jnp.broadcast_to(b_col[end:end + 1, :] + u_end, m_ref.shape)


def _scan_kernel(qf, kf, vf, gcf, grf, qb, kb, vb, gcb, grb, hf_ref, hb_ref, cx_scr, m_scr):
    L = qf.shape[0]

    @pl.when(pl.program_id(1) == 0)
    def _():
        cx_scr[...] = jnp.zeros_like(cx_scr)
        m_scr[...] = jnp.zeros_like(m_scr)

    row = lax.broadcasted_iota(jnp.int32, (L, L), 0)
    col = lax.broadcasted_iota(jnp.int32, (L, L), 1)
    ones = jnp.ones((L, LANES), BF16)
    directions = (
        (qf, kf, vf, gcf, grf, hf_ref, col <= row, 0, L - 1),
        (qb, kb, vb, gcb, grb, hb_ref, col >= row, 2 * MH, 0),
    )
    for d, (q_ref, k_ref, v_ref, gc_ref, gr_ref, h_ref, mask, goff, end) in enumerate(directions):
        for h in range(MH):
            ig, fg = goff + h, goff + MH + h
            _scan_unit(
                q_ref[:, h * MDK:(h + 1) * MDK], k_ref[:, h * MDK:(h + 1) * MDK],
                jnp.concatenate([v_ref[:, h * MDV:(h + 1) * MDV], ones], axis=1),
                gr_ref[ig:ig + 1, :], gc_ref[:, ig:ig + 1], gc_ref[:, fg:fg + 1],
                mask, end, cx_scr.at[d * MH + h], m_scr.at[d * MH + h],
                h_ref.at[:, h * MDV:(h + 1) * MDV])


def _mlstm_scan(p, gc, gr):
    L = L_SCAN
    n_ctx_chunks = CTX // L
    n_lat_chunks = SEQ // L
    steps = n_ctx_chunks + n_lat_chunks
    ctx0 = N_LAT // L

    def fwd(b, s):
        return jnp.where(s < n_ctx_chunks, ctx0 + b * n_ctx_chunks + s,
                         b * n_lat_chunks + s - n_ctx_chunks)

    def bwd(b, s):
        return jnp.where(s < n_ctx_chunks, ctx0 + b * n_ctx_chunks + (n_ctx_chunks - 1 - s),
                         b * n_lat_chunks + (n_lat_chunks - 1 - (s - n_ctx_chunks)))

    def specs(idx):
        return [
            pl.BlockSpec((L, MQK), lambda b, s: (idx(b, s), 0)),
            pl.BlockSpec((L, MQK), lambda b, s: (idx(b, s), 1)),
            pl.BlockSpec((L, MV), lambda b, s: (idx(b, s), 1)),
            pl.BlockSpec((L, GATE_PAD), lambda b, s: (idx(b, s), 0)),
            pl.BlockSpec((M_GATES, L), lambda b, s: (0, idx(b, s))),
        ]

    return pl.pallas_call(
        _scan_kernel,
        out_shape=(jax.ShapeDtypeStruct((N_ALL, MV), F32),
                   jax.ShapeDtypeStruct((N_ALL, MV), F32)),
        grid=(BATCH, steps),
        in_specs=specs(fwd) + specs(bwd),
        out_specs=(pl.BlockSpec((L, MV), lambda b, s: (fwd(b, s), 0)),
                   pl.BlockSpec((L, MV), lambda b, s: (bwd(b, s), 0))),
        scratch_shapes=[pltpu.VMEM((2 * MH, MDK, MDV + LANES), F32),
                        pltpu.VMEM((2 * MH, 8, LANES), F32)],
        compiler_params=_params(2),
        name="mlstm_scan",
    )(p, p, p, gc, gr, p, p, p, gc, gr)


def _mlstm_out_kernel(hf_ref, hb_ref, o_ref, xl_ref, xc_ref, gh_ref, gp_ref, gt_ref, w_ref,
                      out_ref, a_scr):
    i = pl.program_id(0)
    for h in range(MH):
        sl = slice(h * MDV, (h + 1) * MDV)
        hh = hf_ref[:, sl] + hb_ref[:, sl]
        r = lax.rsqrt(jnp.mean(hh * hh, axis=-1, keepdims=True) + EPS)
        hn = (hh * r) * gh_ref[:, sl]
        a_scr[:, sl] = (_sigmoid(o_ref[:, sl].astype(F32)) * hn).astype(BF16)
    y = jnp.dot(a_scr[...], w_ref[...], preferred_element_type=F32)
    x = jnp.where(i < N_LAT // TM_OUT, xl_ref[...], xc_ref[...])
    out_ref[...] = _post_norm_residual(x, y, gp_ref[...], gt_ref[...])


def _mlstm_out(hf, hb, p, xl, xc, g_head, g_post, mod, w_out):
    tm = TM_OUT
    vec = pl.BlockSpec((1, D), lambda i: (0, 0))
    return pl.pallas_call(
        _mlstm_out_kernel,
        out_shape=jax.ShapeDtypeStruct((N_ALL, D), F32),
        grid=(N_ALL // tm,),
        in_specs=[
            pl.BlockSpec((tm, MV), lambda i: (i, 0)),
            pl.BlockSpec((tm, MV), lambda i: (i, 0)),
            pl.BlockSpec((tm, MV), lambda i: (i, 2)),
        ] + _two_source_specs(tm) + [
            vec, vec,
            pl.BlockSpec((None, 1, D), lambda i: (_mod_row(i, tm), 0, 2)),
            pl.BlockSpec((MV, D), lambda i: (0, 0)),
        ],
        out_specs=pl.BlockSpec((tm, D), lambda i: (i, 0)),
        scratch_shapes=[pltpu.VMEM((tm, MV), BF16)],
        compiler_params=_params(1),
        name="mlstm_out",
    )(hf, hb, p, xl, xc, g_head, g_post, mod, w_out)


def _ffn_kernel(x_ref, g_ref, sh_ref, sc_ref, gt_ref, gp_ref, wg_ref, wu_ref, wo_ref, out_ref,
                h_scr, acc_scr):
    f = pl.program_id(1)

    @pl.when(f == 0)
    def _():
        h_scr[...] = _adaln(x_ref[...], g_ref[...], sh_ref[...], sc_ref[...]).astype(BF16)
        acc_scr[...] = jnp.zeros_like(acc_scr)

    h = h_scr[...]
    gate = jnp.dot(h, wg_ref[...], preferred_element_type=F32)
    up = jnp.dot(h, wu_ref[...], preferred_element_type=F32)
    act = ((gate * _sigmoid(gate)) * up).astype(BF16)
    acc_scr[...] += jnp.dot(act, wo_ref[...], preferred_element_type=F32)

    @pl.when(f == pl.num_programs(1) - 1)
    def _():
        out_ref[...] = _post_norm_residual(x_ref[...], acc_scr[...], gp_ref[...], gt_ref[...])


def _ffn(x_rows, g_pre, g_post, mod, w_in, w_out):
    n_rows = x_rows.shape[0]
    tm, tf = TM_FFN, TF_FFN
    nf = FFN // tf
    vec = pl.BlockSpec((1, D), lambda i, f: (0, 0))
    return pl.pallas_call(
        _ffn_kernel,
        out_shape=jax.ShapeDtypeStruct((n_rows, D), F32),
        grid=(n_rows // tm, nf),
        in_specs=[
            pl.BlockSpec((tm, D), lambda i, f: (i, 0)),
            vec, _mod_spec(tm, 3), _mod_spec(tm, 4), _mod_spec(tm, 5), vec,
            pl.BlockSpec((D, tf), lambda i, f: (0, f)),
            pl.BlockSpec((D, tf), lambda i, f: (0, nf + f)),
            pl.BlockSpec((tf, D), lambda i, f: (f, 0)),
        ],
        out_specs=pl.BlockSpec((tm, D), lambda i, f: (i, 0)),
        scratch_shapes=[pltpu.VMEM((tm, D), BF16), pltpu.VMEM((tm, D), F32)],
        compiler_params=_params(2),
        name="ffn",
    )(x_rows, g_pre, mod, mod, mod, g_post, w_in, w_in, w_out)


def _norm_rope_pair(h_bf16, w_ref, cols, a, b):
    y = jnp.dot(h_bf16, w_ref[:, cols], preferred_element_type=F32)
    outs = []
    for h in range(2):
        yy = y[:, h * ADH:(h + 1) * ADH]
        r = lax.rsqrt(jnp.mean(yy * yy, axis=-1, keepdims=True) + EPS)
        outs.append(((yy * a + pltpu.roll(yy, ADH // 2, 1) * b) * r).astype(BF16))
    return jnp.concatenate(outs, axis=1)


def _qkv_kernel(x_ref, g_ref, sh_ref, sc_ref, aq_ref, bq_ref, ak_ref, bk_ref, w_ref,
                q_ref, k_ref, vt_ref, h_scr):
    j = pl.program_id(1)
    n_q = AH * ADH // TN_QKV
    kv_cols = AKV * ADH
    pair = 2 * ADH

    @pl.when(j == 0)
    def _():
        h_scr[...] = _adaln(x_ref[...], g_ref[...], sh_ref[...], sc_ref[...]).astype(BF16)

    @pl.when(j < n_q)
    def _():
        h = h_scr[...]
        for c in range(0, TN_QKV, pair):
            q_ref[:, c:c + pair] = _norm_rope_pair(h, w_ref, slice(c, c + pair),
                                                   aq_ref[...], bq_ref[...])

    @pl.when(j == n_q)
    def _():
        h = h_scr[...]
        for c in range(0, kv_cols, pair):
            k_ref[:, c:c + pair] = _norm_rope_pair(h, w_ref, slice(c, c + pair),
                                                   ak_ref[...], bk_ref[...])
        v = jnp.dot(h, w_ref[:, kv_cols:], preferred_element_type=F32)
        vt = v.T
        ones = jnp.ones((VT_ONES, vt.shape[1]), BF16)
        for hd in range(AKV):
            vt_ref[hd * VT_ROWS:hd * VT_ROWS + ADH, :] = vt[hd * ADH:(hd + 1) * ADH, :].astype(BF16)
            vt_ref[hd * VT_ROWS + ADH:(hd + 1) * VT_ROWS, :] = ones


def _attn_qkv(x_all, gain, mod, a_q, b_q, a_k, b_k, w_qkv):
    tm, tn = TM_PROJ, TN_QKV
    n_q = AH * ADH // tn
    n_lat = N_LAT // tm
    per_seq = SEQ // tm
    kv_cols = AKV * ADH
    assert tn == 2 * kv_cols
    vec = pl.BlockSpec((1, D), lambda i, j: (0, 0))
    rope = pl.BlockSpec((tm, ADH), lambda i, j: (jnp.where(i < n_lat, i % per_seq, per_seq), 0))
    return pl.pallas_call(
        _qkv_kernel,
        out_shape=(jax.ShapeDtypeStruct((N_ALL, AH * ADH), BF16),
                   jax.ShapeDtypeStruct((N_ALL, kv_cols), BF16),
                   jax.ShapeDtypeStruct((AKV * VT_ROWS, N_ALL), BF16)),
        grid=(N_ALL // tm, n_q + 1),
        in_specs=[
            pl.BlockSpec((tm, D), lambda i, j: (i, 0)),
            vec, _mod_spec(tm, 0), _mod_spec(tm, 1), rope, rope, rope, rope,
            pl.BlockSpec((D, tn), lambda i, j: (0, j)),
        ],
        out_specs=(pl.BlockSpec((tm, tn), lambda i, j: (i, jnp.minimum(j, n_q - 1))),
                   pl.BlockSpec((tm, kv_cols), lambda i, j: (i, 0)),
                   pl.BlockSpec((AKV * VT_ROWS, tm), lambda i, j: (0, i))),
        scratch_shapes=[pltpu.VMEM((tm, D), BF16)],
        compiler_params=_params(2),
        name="attn_qkv",
    )(x_all, gain, mod, mod, a_q, b_q, a_k, b_k, w_qkv)


def _attn_kernel(q_ref, kl_ref, kc_ref, vtl_ref, vtc_ref, o_ref):
    tq = q_ref.shape[0]
    nq = AG * tq
    q = jnp.concatenate([q_ref[:, g * ADH:(g + 1) * ADH] for g in range(AG)], axis=0)

    def block(k, vt, m, acc):
        st = lax.dot_general(k, q, _NT, preferred_element_type=F32)
        m_new = jnp.maximum(m, jnp.max(st, axis=0, keepdims=True))
        p = jnp.exp2(st - m_new).astype(BF16)
        acc = jnp.exp2(m - m_new) * acc + jnp.dot(vt, p, preferred_element_type=F32)
        return m_new, acc

    m = jnp.full((1, nq), -jnp.inf, F32)
    acc = jnp.zeros((VT_ROWS, nq), F32)
    for c in range(SEQ // TK_ATT):
        m, acc = block(kl_ref[c * TK_ATT:(c + 1) * TK_ATT, :],
                       vtl_ref[:, c * TK_ATT:(c + 1) * TK_ATT], m, acc)
    m, acc = block(kc_ref[...], vtc_ref[...], m, acc)
    out = (acc[:ADH, :] * (1.0 / acc[ADH:ADH + 1, :])).T
    for g in range(AG):
        o_ref[:, g * ADH:(g + 1) * ADH] = out[g * tq:(g + 1) * tq, :].astype(BF16)


def _attention(q, k, vt):
    tq = TQ_ATT
    per_seq = SEQ // tq
    return pl.pallas_call(
        _attn_kernel,
        out_shape=jax.ShapeDtypeStruct((N_LAT, AH * ADH), BF16),
        grid=(BATCH, AKV, per_seq),
        in_specs=[
            pl.BlockSpec((tq, AG * ADH), lambda b, h, i: (b * per_seq + i, h)),
            pl.BlockSpec((SEQ, ADH), lambda b, h, i: (b, h)),
            pl.BlockSpec((CTX, ADH), lambda b, h, i: (N_LAT // CTX + b, h)),
            pl.BlockSpec((VT_ROWS, SEQ), lambda b, h, i: (h, b)),
            pl.BlockSpec((VT_ROWS, CTX), lambda b, h, i: (h, N_LAT // CTX + b)),
        ],
        out_specs=pl.BlockSpec((tq, AG * ADH), lambda b, h, i: (b * per_seq + i, h)),
        compiler_params=_params(3),
        name="attention",
    )(q, k, k, vt, vt)


def _attn_out_kernel(o_ref, x_ref, gp_ref, gt_ref, w_ref, out_ref):
    for r in range(0, TM_ATT_OUT, TM_OUT):
        rows = slice(r, r + TM_OUT)
        y = jnp.dot(o_ref[rows, :], w_ref[...], preferred_element_type=F32)
        out_ref[rows, :] = _post_norm_residual(x_ref[rows, :], y, gp_ref[...], gt_ref[...])


def _attn_out(o, x_all, g_post, mod, w_out):
    tm = TM_ATT_OUT
    return pl.pallas_call(
        _attn_out_kernel,
        out_shape=jax.ShapeDtypeStruct((N_LAT, D), F32),
        grid=(N_LAT // tm,),
        in_specs=[
            pl.BlockSpec((tm, D), lambda i: (i, 0)),
            pl.BlockSpec((tm, D), lambda i: (i, 0)),
            pl.BlockSpec((1, D), lambda i: (0, 0)),
            pl.BlockSpec((None, 1, D), lambda i: (_mod_row(i, tm), 0, 2)),
            pl.BlockSpec((D, D), lambda i: (0, 0)),
        ],
        out_specs=pl.BlockSpec((tm, D), lambda i: (i, 0)),
        compiler_params=_params(1),
        name="attn_out",
    )(o, x_all, g_post, mod, w_out)


def _half_split(a):
    lead = a.shape[:-1]
    n = a.shape[-1] // ADH
    a = a.reshape(*lead, n, ADH // 2, 2)
    return jnp.swapaxes(a, -1, -2).reshape(*lead, n * ADH)


def _rope_tables(gain, scale):
    rows = SEQ // GRID_W
    t_row = jnp.repeat(jnp.arange(rows, dtype=F32), GRID_W)
    t_col = jnp.tile(jnp.arange(GRID_W, dtype=F32), rows)
    per_axis = ADH // 2
    inv = ROPE_THETA ** (-jnp.arange(0, per_axis, 2, dtype=F32) / per_axis)
    ang = jnp.concatenate([t_row[:, None] * inv, t_col[:, None] * inv], axis=-1)
    ang = jnp.concatenate([ang, jnp.zeros((TM_PROJ, per_axis), F32)], axis=0)
    cos = jnp.concatenate([jnp.cos(ang), jnp.cos(ang)], axis=-1)
    sin = jnp.concatenate([-jnp.sin(ang), jnp.sin(ang)], axis=-1)
    g = _half_split(gain.astype(F32))
    return (g * scale) * cos, (jnp.roll(g, per_axis) * scale) * sin


def kernel(x, c, ctx, c_ctx, w_mod, b_mod, g_mix_pre, g_mix_post, g_ffn_pre, g_ffn_post,
           w_mlstm_in, b_mlstm_gate, g_mlstm_head, w_mlstm_out,
           w_attn_qkv, g_attn_q, g_attn_k, w_attn_out, w_ffn_in, w_ffn_out):
    assert x.shape == (BATCH, SEQ, D) and ctx.shape == (BATCH, CTX, D)
    xl = x.reshape(N_LAT, D)
    xc = ctx.reshape(N_CTX, D)
    row = lambda a: a.reshape(1, -1)

    c_all = jnp.concatenate([c, c_ctx[None, :], jnp.zeros((MOD_ROWS - BATCH - 1, D), F32)], axis=0)
    mod = _modulation(c_all, w_mod, b_mod).reshape(2, MOD_ROWS, 1, N_MOD * D)

    w_in = w_mlstm_in[0]
    w_main = w_in[:, :M_MAIN].astype(BF16)
    w_gate = jnp.pad(w_in[:, M_MAIN:], ((0, 0), (0, GATE_PAD - M_GATES))).astype(BF16)
    b_gate = jnp.pad(b_mlstm_gate[0], (0, GATE_PAD - M_GATES)).reshape(1, GATE_PAD)
    p, gpre = _mlstm_inproj(xl, xc, row(g_mix_pre[0]), mod[0], w_main, w_gate)
    gc, gr = _gateprep(gpre, b_gate)
    hf, hb = _mlstm_scan(p, gc, gr)
    x_all = _mlstm_out(hf, hb, p, xl, xc, row(g_mlstm_head[0]), row(g_mix_post[0]), mod[0],
                       w_mlstm_out[0].astype(BF16))
    x_all = _ffn(x_all, row(g_ffn_pre[0]), row(g_ffn_post[0]), mod[0],
                 w_ffn_in[0].astype(BF16), w_ffn_out[0].astype(BF16))

    n_qk = (AH + AKV) * ADH
    w_qkv = jnp.concatenate([_half_split(w_attn_qkv[0][:, :n_qk]), w_attn_qkv[0][:, n_qk:]],
                            axis=1).astype(BF16)
    a_q, b_q = _rope_tables(g_attn_q[0], A_SCALE * LOG2E)
    a_k, b_k = _rope_tables(g_attn_k[0], 1.0)
    q, k, vt = _attn_qkv(x_all, row(g_mix_pre[1]), mod[1], a_q, b_q, a_k, b_k, w_qkv)
    o = _attention(q, k, vt)
    x_lat = _attn_out(o, x_all, row(g_mix_post[1]), mod[1], w_attn_out[0].astype(BF16))
    x_lat = _ffn(x_lat, row(g_ffn_pre[1]), row(g_ffn_post[1]), mod[1],
                 w_ffn_in[1].astype(BF16), w_ffn_out[1].astype(BF16))
    return x_lat.reshape(BATCH, SEQ, D)
```

```python
import functools
import math

import jax
import jax.numpy as jnp
from jax import lax
from jax.experimental import pallas as pl
from jax.experimental.pallas import tpu as pltpu

F32 = jnp.float32
BF16 = jnp.bfloat16

D = 2048
BATCH = 4
SEQ = 4096
CTX = 256
N_LAT = BATCH * SEQ
N_CTX = BATCH * CTX
N_ALL = N_LAT + N_CTX
N_MOD = 6
EPS = 1e-6
MOD_ROWS = 8

MH = 8
MDK = 128
MDV = 256
MQK = MH * MDK
MV = MH * MDV
M_MAIN = 2 * MQK + 2 * MV
M_GATES = 4 * MH
GATE_PAD = 128
SOFTCAP = 15.0
M_SCALE_LOG2 = -0.5 * math.log2(MDK)
LANES = 128

AH = 16
AKV = 4
ADH = 128
AG = AH // AKV
A_SCALE = ADH ** -0.5
LOG2E = math.log2(math.e)
VT_ONES = 16
VT_ROWS = ADH + VT_ONES
GRID_W = 64
ROPE_THETA = 10000.0

FFN = 5632

VMEM_LIMIT = 56 * 1024 * 1024
TM_PROJ = 512
TN_INPROJ = 2048
TN_QKV = 1024
TM_OUT = 256
TM_ATT_OUT = 2 * TM_OUT
TM_MLSTM_OUT = 2 * TM_OUT
TM_FFN = 1024
TF_FFN = 512
TN_MOD = 1024
L_SCAN = 256
TQ_ATT = 512
TK_ATT = 1024

_NT = (((1,), (1,)), ((), ()))
_TN = (((0,), (0,)), ((), ()))


def _params(n_axes):
    return pltpu.CompilerParams(
        dimension_semantics=("arbitrary",) * n_axes, vmem_limit_bytes=VMEM_LIMIT)


def _mod_row(i, tm):
    return jnp.where(i < N_LAT // tm, i // (SEQ // tm), BATCH)


def _mod_spec(tm, chunk):
    return pl.BlockSpec((None, 1, D), lambda i, j: (_mod_row(i, tm), 0, chunk))


def _adaln(x, gain, shift, scale):
    r = lax.rsqrt(jnp.mean(x * x, axis=-1, keepdims=True) + EPS)
    return (x * r) * (gain * (1.0 + scale)) + shift


def _post_norm_residual(x, y, gain, gate):
    r = lax.rsqrt(jnp.mean(y * y, axis=-1, keepdims=True) + EPS)
    return x + gate * ((y * r) * gain)


def _sigmoid(z):
    return 1.0 / (1.0 + jnp.exp(-z))


def _mod_kernel(c_ref, w_ref, b_ref, o_ref):
    c = c_ref[...]
    cond = (c * _sigmoid(c)).astype(BF16)
    o_ref[...] = jnp.dot(cond, w_ref[...].astype(BF16), preferred_element_type=F32) + b_ref[...]


def _modulation(c_all, w_mod, b_mod):
    depth = w_mod.shape[0]
    return pl.pallas_call(
        _mod_kernel,
        out_shape=jax.ShapeDtypeStruct((depth, MOD_ROWS, N_MOD * D), F32),
        grid=(depth, N_MOD * D // TN_MOD),
        in_specs=[
            pl.BlockSpec((MOD_ROWS, D), lambda l, j: (0, 0)),
            pl.BlockSpec((None, D, TN_MOD), lambda l, j: (l, 0, j)),
            pl.BlockSpec((None, 1, TN_MOD), lambda l, j: (l, 0, j)),
        ],
        out_specs=pl.BlockSpec((None, MOD_ROWS, TN_MOD), lambda l, j: (l, 0, j)),
        compiler_params=_params(2),
        name="modulation",
    )(c_all, w_mod, b_mod.reshape(depth, 1, N_MOD * D))


def _inproj_kernel(xl_ref, xc_ref, g_ref, sh_ref, sc_ref, w_ref, wg_ref, p_ref, gpre_ref, h_scr):
    i = pl.program_id(0)
    j = pl.program_id(1)

    @pl.when(j == 0)
    def _():
        is_latent = i < N_LAT // TM_PROJ
        half = TM_PROJ // 2
        for r in range(0, TM_PROJ, half):
            rows = slice(r, r + half)
            x = jnp.where(is_latent, xl_ref[rows, :], xc_ref[rows, :])
            h = _adaln(x, g_ref[...], sh_ref[...], sc_ref[...]).astype(BF16)
            h_scr[rows, :] = h
            gpre_ref[rows, :] = jnp.dot(h, wg_ref[...], preferred_element_type=F32)
            p_ref[rows, :] = jnp.dot(h, w_ref[...], preferred_element_type=F32).astype(BF16)

    @pl.when(j > 0)
    def _():
        p_ref[...] = jnp.dot(h_scr[...], w_ref[...], preferred_element_type=F32).astype(BF16)


def _two_source_specs(tm):
    n_lat = N_LAT // tm
    return [
        pl.BlockSpec((tm, D), lambda i, *_: (jnp.minimum(i, n_lat - 1), 0)),
        pl.BlockSpec((tm, D), lambda i, *_: (jnp.maximum(i - n_lat, 0), 0)),
    ]


def _mlstm_inproj(xl, xc, gain, mod, w_main, w_gate):
    tm, tn = TM_PROJ, TN_INPROJ
    vec = pl.BlockSpec((1, D), lambda i, j: (0, 0))
    return pl.pallas_call(
        _inproj_kernel,
        out_shape=(jax.ShapeDtypeStruct((N_ALL, M_MAIN), BF16),
                   jax.ShapeDtypeStruct((N_ALL, GATE_PAD), F32)),
        grid=(N_ALL // tm, M_MAIN // tn),
        in_specs=_two_source_specs(tm) + [
            vec, _mod_spec(tm, 0), _mod_spec(tm, 1),
            pl.BlockSpec((D, tn), lambda i, j: (0, j)),
            pl.BlockSpec((D, GATE_PAD), lambda i, j: (0, 0)),
        ],
        out_specs=(pl.BlockSpec((tm, tn), lambda i, j: (i, j)),
                   pl.BlockSpec((tm, GATE_PAD), lambda i, j: (i, 0))),
        scratch_shapes=[pltpu.VMEM((tm, D), BF16)],
        compiler_params=_params(2),
        name="mlstm_inproj",
    )(xl, xc, gain, mod, mod, w_main, w_gate)


def _gateprep_kernel(gpre_ref, b_ref, gc_ref, gr_ref):
    L = gpre_ref.shape[0]
    z = gpre_ref[...] + b_ref[...]
    a = SOFTCAP * jnp.tanh(z * (1.0 / SOFTCAP))
    logsig = jnp.minimum(a, 0.0) - jnp.log(1.0 + jnp.exp(-jnp.abs(a)))
    row = lax.broadcasted_iota(jnp.int32, (L, GATE_PAD), 0)
    lane = lax.broadcasted_iota(jnp.int32, (L, GATE_PAD), 1)

    def scans(x, op, fill):
        pre, suf = x, x
        s = 1
        while s < L:
            pre = op(pre, jnp.where(row >= s, pltpu.roll(pre, s, 0), fill))
            suf = op(suf, jnp.where(row < L - s, pltpu.roll(suf, L - s, 0), fill))
            s *= 2
        return pre, suf

    fwd_lanes = lane < 2 * MH
    b_pre, b_suf = scans(logsig, jnp.add, 0.0)
    b = jnp.where(fwd_lanes, b_pre, b_suf)
    r = a - pltpu.roll(b, GATE_PAD - MH, 1)
    c_pre, c_suf = scans(r, jnp.maximum, -jnp.inf)
    cmax = jnp.where(fwd_lanes, c_pre, c_suf)
    is_gate_lane = (lane < MH) | ((lane >= 2 * MH) & (lane < 3 * MH))
    gc_ref[...] = jnp.where(is_gate_lane, cmax, b) * LOG2E
    gr_ref[...] = (r * LOG2E).T[:M_GATES, :]


def _gateprep(gpre, bias):
    L = L_SCAN
    return pl.pallas_call(
        _gateprep_kernel,
        out_shape=(jax.ShapeDtypeStruct((N_ALL, GATE_PAD), F32),
                   jax.ShapeDtypeStruct((M_GATES, N_ALL), F32)),
        grid=(N_ALL // L,),
        in_specs=[pl.BlockSpec((L, GATE_PAD), lambda i: (i, 0)),
                  pl.BlockSpec((1, GATE_PAD), lambda i: (0, 0))],
        out_specs=(pl.BlockSpec((L, GATE_PAD), lambda i: (i, 0)),
                   pl.BlockSpec((M_GATES, L), lambda i: (0, i))),
        compiler_params=_params(1),
        name="mlstm_gateprep",
    )(gpre, bias)


def _scan_unit(q, k, vx, r_row, cmax_col, b_col, mask, end, cx_ref, m_ref, h_out):
    L = q.shape[0]
    m = m_ref[0:1, 0:1]
    u = jnp.maximum(m, cmax_col)
    u_b = jnp.broadcast_to(u - M_SCALE_LOG2, (L, L))
    dmat = jnp.where(mask, jnp.exp2(r_row - u_b), 0.0)
    sm = (lax.dot_general(q, k, _NT, preferred_element_type=F32) * dmat).astype(BF16)
    cx = cx_ref[...]
    intra = jnp.dot(sm, vx, preferred_element_type=F32)
    inter = jnp.dot(q, cx.astype(BF16), preferred_element_type=F32)
    u_rep = u_b[:, :LANES]
    w_inter = jnp.exp2(m - u_rep)
    den = intra[:, MDV:] + w_inter * inter[:, MDV:]
    floor = jnp.exp2(-(jnp.broadcast_to(b_col, (L, LANES)) + u_rep + M_SCALE_LOG2))
    inv = 1.0 / jnp.maximum(jnp.abs(den), floor)
    for t in range(MDV // LANES):
        cols = slice(t * LANES, (t + 1) * LANES)
        h_out[:, cols] = ((intra[:, cols] + w_inter * inter[:, cols]) * inv).astype(h_out.dtype)

    u_end = u[end:end + 1, :]
    w_row = jnp.exp2(r_row - u_end)
    kw = (k.astype(F32).T * w_row).astype(BF16)
    cx_ref[...] = jnp.exp2(m - u_end) * cx + jnp.dot(kw, vx, preferred_element_type=F32)
    m_ref[...] = jnp.broadcast_to(b_col[end:end + 1, :] + u_end, m_ref.shape)


def _scan_kernel(qf, kf, vf, gcf, grf, qb, kb, vb, gcb, grb, hf_ref, hb_ref, cx_scr, m_scr):
    L = qf.shape[0]

    @pl.when(pl.program_id(1) == 0)
    def _():
        cx_scr[...] = jnp.zeros_like(cx_scr)
        m_scr[...] = jnp.zeros_like(m_scr)

    row = lax.broadcasted_iota(jnp.int32, (L, L), 0)
    col = lax.broadcasted_iota(jnp.int32, (L, L), 1)
    ones = jnp.ones((L, LANES), BF16)
    directions = (
        (qf, kf, vf, gcf, grf, hf_ref, col <= row, 0, L - 1),
        (qb, kb, vb, gcb, grb, hb_ref, col >= row, 2 * MH, 0),
    )
    for d, (q_ref, k_ref, v_ref, gc_ref, gr_ref, h_ref, mask, goff, end) in enumerate(directions):
        for h in range(MH):
            ig, fg = goff + h, goff + MH + h
            _scan_unit(
                q_ref[:, h * MDK:(h + 1) * MDK], k_ref[:, h * MDK:(h + 1) * MDK],
                jnp.concatenate([v_ref[:, h * MDV:(h + 1) * MDV], ones], axis=1),
                gr_ref[ig:ig + 1, :], gc_ref[:, ig:ig + 1], gc_ref[:, fg:fg + 1],
                mask, end, cx_scr.at[d * MH + h], m_scr.at[d * MH + h],
                h_ref.at[:, h * MDV:(h + 1) * MDV])


def _mlstm_scan(p, gc, gr):
    L = L_SCAN
    n_ctx_chunks = CTX // L
    n_lat_chunks = SEQ // L
    steps = n_ctx_chunks + n_lat_chunks
    ctx0 = N_LAT // L

    def fwd(b, s):
        return jnp.where(s < n_ctx_chunks, ctx0 + b * n_ctx_chunks + s,
                         b * n_lat_chunks + s - n_ctx_chunks)

    def bwd(b, s):
        return jnp.where(s < n_ctx_chunks, ctx0 + b * n_ctx_chunks + (n_ctx_chunks - 1 - s),
                         b * n_lat_chunks + (n_lat_chunks - 1 - (s - n_ctx_chunks)))

    def specs(idx):
        return [
            pl.BlockSpec((L, MQK), lambda b, s: (idx(b, s), 0)),
            pl.BlockSpec((L, MQK), lambda b, s: (idx(b, s), 1)),
            pl.BlockSpec((L, MV), lambda b, s: (idx(b, s), 1)),
            pl.BlockSpec((L, GATE_PAD), lambda b, s: (idx(b, s), 0)),
            pl.BlockSpec((M_GATES, L), lambda b, s: (0, idx(b, s))),
        ]

    return pl.pallas_call(
        _scan_kernel,
        out_shape=(jax.ShapeDtypeStruct((N_ALL, MV), BF16),
                   jax.ShapeDtypeStruct((N_ALL, MV), BF16)),
        grid=(BATCH, steps),
        in_specs=specs(fwd) + specs(bwd),
        out_specs=(pl.BlockSpec((L, MV), lambda b, s: (fwd(b, s), 0)),
                   pl.BlockSpec((L, MV), lambda b, s: (bwd(b, s), 0))),
        scratch_shapes=[pltpu.VMEM((2 * MH, MDK, MDV + LANES), F32),
                        pltpu.VMEM((2 * MH, 8, LANES), F32)],
        compiler_params=_params(2),
        name="mlstm_scan",
    )(p, p, p, gc, gr, p, p, p, gc, gr)


def _mlstm_out_kernel(hf_ref, hb_ref, o_ref, xl_ref, xc_ref, gh_ref, gp_ref, gt_ref, w_ref,
                      out_ref, a_scr):
    is_latent = pl.program_id(0) < N_LAT // TM_MLSTM_OUT
    for r0 in range(0, TM_MLSTM_OUT, TM_OUT):
        rows = slice(r0, r0 + TM_OUT)
        for h in range(MH):
            sl = slice(h * MDV, (h + 1) * MDV)
            hh = hf_ref[rows, sl].astype(F32) + hb_ref[rows, sl].astype(F32)
            r = lax.rsqrt(jnp.mean(hh * hh, axis=-1, keepdims=True) + EPS)
            hn = (hh * r) * gh_ref[:, sl]
            a_scr[rows, sl] = (_sigmoid(o_ref[rows, sl].astype(F32)) * hn).astype(BF16)
        y = jnp.dot(a_scr[rows, :], w_ref[...], preferred_element_type=F32)
        x = jnp.where(is_latent, xl_ref[rows, :], xc_ref[rows, :])
        out_ref[rows, :] = _post_norm_residual(x, y, gp_ref[...], gt_ref[...])


def _mlstm_out(hf, hb, p, xl, xc, g_head, g_post, mod, w_out):
    tm = TM_MLSTM_OUT
    vec = pl.BlockSpec((1, D), lambda i: (0, 0))
    return pl.pallas_call(
        _mlstm_out_kernel,
        out_shape=jax.ShapeDtypeStruct((N_ALL, D), F32),
        grid=(N_ALL // tm,),
        in_specs=[
            pl.BlockSpec((tm, MV), lambda i: (i, 0)),
            pl.BlockSpec((tm, MV), lambda i: (i, 0)),
            pl.BlockSpec((tm, MV), lambda i: (i, 2)),
        ] + _two_source_specs(tm) + [
            vec, vec,
            pl.BlockSpec((None, 1, D), lambda i: (_mod_row(i, tm), 0, 2)),
            pl.BlockSpec((MV, D), lambda i: (0, 0), pipeline_mode=pl.Buffered(1)),
        ],
        out_specs=pl.BlockSpec((tm, D), lambda i: (i, 0)),
        scratch_shapes=[pltpu.VMEM((tm, MV), BF16)],
        compiler_params=_params(1),
        name="mlstm_out",
    )(hf, hb, p, xl, xc, g_head, g_post, mod, w_out)


def _ffn_kernel(x_ref, g_ref, sh_ref, sc_ref, gt_ref, gp_ref, wg_ref, wu_ref, wo_ref, out_ref,
                h_scr):
    f = pl.program_id(1)
    last_f = pl.num_programs(1) - 1
    half = TM_FFN // 2

    def step(first, last):
        for r in range(0, TM_FFN, half):
            rows = slice(r, r + half)
            if first:
                h_scr[rows, :] = _adaln(x_ref[rows, :], g_ref[...], sh_ref[...],
                                        sc_ref[...]).astype(BF16)
            h = h_scr[rows, :]
            gate = jnp.dot(h, wg_ref[...], preferred_element_type=F32)
            up = jnp.dot(h, wu_ref[...], preferred_element_type=F32)
            act = ((gate * _sigmoid(gate)) * up).astype(BF16)
            y = jnp.dot(act, wo_ref[...], preferred_element_type=F32)
            if not first:
                y = out_ref[rows, :] + y
            if last:
                y = _post_norm_residual(x_ref[rows, :], y, gp_ref[...], gt_ref[...])
            out_ref[rows, :] = y

    pl.when(f == 0)(functools.partial(step, True, False))
    pl.when((f > 0) & (f < last_f))(functools.partial(step, False, False))
    pl.when(f == last_f)(functools.partial(step, False, True))


def _ffn(x_rows, g_pre, g_post, mod, w_in, w_out):
    n_rows = x_rows.shape[0]
    tm, tf = TM_FFN, TF_FFN
    nf = FFN // tf
    assert nf >= 2
    vec = pl.BlockSpec((1, D), lambda i, f: (0, 0))
    return pl.pallas_call(
        _ffn_kernel,
        out_shape=jax.ShapeDtypeStruct((n_rows, D), F32),
        grid=(n_rows // tm, nf),
        in_specs=[
            pl.BlockSpec((tm, D), lambda i, f: (i, 0)),
            vec, _mod_spec(tm, 3), _mod_spec(tm, 4), _mod_spec(tm, 5), vec,
            pl.BlockSpec((D, tf), lambda i, f: (0, f)),
            pl.BlockSpec((D, tf), lambda i, f: (0, nf + f)),
            pl.BlockSpec((tf, D), lambda i, f: (f, 0)),
        ],
        out_specs=pl.BlockSpec((tm, D), lambda i, f: (i, 0)),
        scratch_shapes=[pltpu.VMEM((tm, D), BF16)],
        compiler_params=_params(2),
        name="ffn",
    )(x_rows, g_pre, mod, mod, mod, g_post, w_in, w_in, w_out)


def _norm_rope_pair(h_bf16, w_ref, cols, a, b):
    y = jnp.dot(h_bf16, w_ref[:, cols], preferred_element_type=F32)
    outs = []
    for h in range(2):
        yy = y[:, h * ADH:(h + 1) * ADH]
        r = lax.rsqrt(jnp.mean(yy * yy, axis=-1, keepdims=True) + EPS)
        outs.append(((yy * a + pltpu.roll(yy, ADH // 2, 1) * b) * r).astype(BF16))
    return jnp.concatenate(outs, axis=1)


def _qkv_kernel(x_ref, g_ref, sh_ref, sc_ref, aq_ref, bq_ref, ak_ref, bk_ref, w_ref,
                q_ref, k_ref, vt_ref, h_scr):
    j = pl.program_id(1)
    n_q = AH * ADH // TN_QKV
    kv_cols = AKV * ADH
    pair = 2 * ADH

    @pl.when(j == 0)
    def _():
        h_scr[...] = _adaln(x_ref[...], g_ref[...], sh_ref[...], sc_ref[...]).astype(BF16)

    @pl.when(j < n_q)
    def _():
        h = h_scr[...]
        for c in range(0, TN_QKV, pair):
            q_ref[:, c:c + pair] = _norm_rope_pair(h, w_ref, slice(c, c + pair),
                                                   aq_ref[...], bq_ref[...])

    @pl.when(j == n_q)
    def _():
        h = h_scr[...]
        for c in range(0, kv_cols, pair):
            k_ref[:, c:c + pair] = _norm_rope_pair(h, w_ref, slice(c, c + pair),
                                                   ak_ref[...], bk_ref[...])
        v = jnp.dot(h, w_ref[:, kv_cols:], preferred_element_type=F32)
        vt = v.T
        ones = jnp.ones((VT_ONES, vt.shape[1]), BF16)
        for hd in range(AKV):
            vt_ref[hd * VT_ROWS:hd * VT_ROWS + ADH, :] = vt[hd * ADH:(hd + 1) * ADH, :].astype(BF16)
            vt_ref[hd * VT_ROWS + ADH:(hd + 1) * VT_ROWS, :] = ones


def _attn_qkv(x_all, gain, mod, a_q, b_q, a_k, b_k, w_qkv):
    tm, tn = TM_PROJ, TN_QKV
    n_q = AH * ADH // tn
    n_lat = N_LAT // tm
    per_seq = SEQ // tm
    kv_cols = AKV * ADH
    assert tn == 2 * kv_cols
    vec = pl.BlockSpec((1, D), lambda i, j: (0, 0))
    rope = pl.BlockSpec((tm, ADH), lambda i, j: (jnp.where(i < n_lat, i % per_seq, per_seq), 0))
    return pl.pallas_call(
        _qkv_kernel,
        out_shape=(jax.ShapeDtypeStruct((N_ALL, AH * ADH), BF16),
                   jax.ShapeDtypeStruct((N_ALL, kv_cols), BF16),
                   jax.ShapeDtypeStruct((AKV * VT_ROWS, N_ALL), BF16)),
        grid=(N_ALL // tm, n_q + 1),
        in_specs=[
            pl.BlockSpec((tm, D), lambda i, j: (i, 0)),
            vec, _mod_spec(tm, 0), _mod_spec(tm, 1), rope, rope, rope, rope,
            pl.BlockSpec((D, tn), lambda i, j: (0, j)),
        ],
        out_specs=(pl.BlockSpec((tm, tn), lambda i, j: (i, jnp.minimum(j, n_q - 1))),
                   pl.BlockSpec((tm, kv_cols), lambda i, j: (i, 0)),
                   pl.BlockSpec((AKV * VT_ROWS, tm), lambda i, j: (0, i))),
        scratch_shapes=[pltpu.VMEM((tm, D), BF16)],
        compiler_params=_params(2),
        name="attn_qkv",
    )(x_all, gain, mod, mod, a_q, b_q, a_k, b_k, w_qkv)


def _attn_kernel(q_ref, kl_ref, kc_ref, vtl_ref, vtc_ref, o_ref):
    tq = q_ref.shape[0]
    nq = AG * tq
    q = jnp.concatenate([q_ref[:, g * ADH:(g + 1) * ADH] for g in range(AG)], axis=0)

    def block(k, vt, m, acc):
        st = lax.dot_general(k, q, _NT, preferred_element_type=F32)
        m_new = jnp.maximum(m, jnp.max(st, axis=0, keepdims=True))
        p = jnp.exp2(st - m_new).astype(BF16)
        acc = jnp.exp2(m - m_new) * acc + jnp.dot(vt, p, preferred_element_type=F32)
        return m_new, acc

    m = jnp.full((1, nq), -jnp.inf, F32)
    acc = jnp.zeros((VT_ROWS, nq), F32)
    for c in range(SEQ // TK_ATT):
        m, acc = block(kl_ref[c * TK_ATT:(c + 1) * TK_ATT, :],
                       vtl_ref[:, c * TK_ATT:(c + 1) * TK_ATT], m, acc)
    m, acc = block(kc_ref[...], vtc_ref[...], m, acc)
    out = (acc[:ADH, :] * (1.0 / acc[ADH:ADH + 1, :])).T
    for g in range(AG):
        o_ref[:, g * ADH:(g + 1) * ADH] = out[g * tq:(g + 1) * tq, :].astype(BF16)


def _attention(q, k, vt):
    tq = TQ_ATT
    per_seq = SEQ // tq
    return pl.pallas_call(
        _attn_kernel,
        out_shape=jax.ShapeDtypeStruct((N_LAT, AH * ADH), BF16),
        grid=(BATCH, AKV, per_seq),
        in_specs=[
            pl.BlockSpec((tq, AG * ADH), lambda b, h, i: (b * per_seq + i, h)),
            pl.BlockSpec((SEQ, ADH), lambda b, h, i: (b, h)),
            pl.BlockSpec((CTX, ADH), lambda b, h, i: (N_LAT // CTX + b, h)),
            pl.BlockSpec((VT_ROWS, SEQ), lambda b, h, i: (h, b)),
            pl.BlockSpec((VT_ROWS, CTX), lambda b, h, i: (h, N_LAT // CTX + b)),
        ],
        out_specs=pl.BlockSpec((tq, AG * ADH), lambda b, h, i: (b * per_seq + i, h)),
        compiler_params=_params(3),
        name="attention",
    )(q, k, k, vt, vt)


def _attn_out_kernel(o_ref, x_ref, gp_ref, gt_ref, w_ref, out_ref):
    for r in range(0, TM_ATT_OUT, TM_OUT):
        rows = slice(r, r + TM_OUT)
        y = jnp.dot(o_ref[rows, :], w_ref[...], preferred_element_type=F32)
        out_ref[rows, :] = _post_norm_residual(x_ref[rows, :], y, gp_ref[...], gt_ref[...])


def _attn_out(o, x_all, g_post, mod, w_out):
    tm = TM_ATT_OUT
    return pl.pallas_call(
        _attn_out_kernel,
        out_shape=jax.ShapeDtypeStruct((N_LAT, D), F32),
        grid=(N_LAT // tm,),
        in_specs=[
            pl.BlockSpec((tm, D), lambda i: (i, 0)),
            pl.BlockSpec((tm, D), lambda i: (i, 0)),
            pl.BlockSpec((1, D), lambda i: (0, 0)),
            pl.BlockSpec((None, 1, D), lambda i: (_mod_row(i, tm), 0, 2)),
            pl.BlockSpec((D, D), lambda i: (0, 0)),
        ],
        out_specs=pl.BlockSpec((tm, D), lambda i: (i, 0)),
        compiler_params=_params(1),
        name="attn_out",
    )(o, x_all, g_post, mod, w_out)


def _half_split(a):
    lead = a.shape[:-1]
    n = a.shape[-1] // ADH
    a = a.reshape(*lead, n, ADH // 2, 2)
    return jnp.swapaxes(a, -1, -2).reshape(*lead, n * ADH)


def _rope_tables(gain, scale):
    rows = SEQ // GRID_W
    t_row = jnp.repeat(jnp.arange(rows, dtype=F32), GRID_W)
    t_col = jnp.tile(jnp.arange(GRID_W, dtype=F32), rows)
    per_axis = ADH // 2
    inv = ROPE_THETA ** (-jnp.arange(0, per_axis, 2, dtype=F32) / per_axis)
    ang = jnp.concatenate([t_row[:, None] * inv, t_col[:, None] * inv], axis=-1)
    ang = jnp.concatenate([ang, jnp.zeros((TM_PROJ, per_axis), F32)], axis=0)
    cos = jnp.concatenate([jnp.cos(ang), jnp.cos(ang)], axis=-1)
    sin = jnp.concatenate([-jnp.sin(ang), jnp.sin(ang)], axis=-1)
    g = _half_split(gain.astype(F32))
    return (g * scale) * cos, (jnp.roll(g, per_axis) * scale) * sin


def kernel(x, c, ctx, c_ctx, w_mod, b_mod, g_mix_pre, g_mix_post, g_ffn_pre, g_ffn_post,
           w_mlstm_in, b_mlstm_gate, g_mlstm_head, w_mlstm_out,
           w_attn_qkv, g_attn_q, g_attn_k, w_attn_out, w_ffn_in, w_ffn_out):
    assert x.shape == (BATCH, SEQ, D) and ctx.shape == (BATCH, CTX, D)
    xl = x.reshape(N_LAT, D)
    xc = ctx.reshape(N_CTX, D)
    row = lambda a: a.reshape(1, -1)

    c_all = jnp.concatenate([c, c_ctx[None, :], jnp.zeros((MOD_ROWS - BATCH - 1, D), F32)], axis=0)
    mod = _modulation(c_all, w_mod, b_mod).reshape(2, MOD_ROWS, 1, N_MOD * D)

    w_main = w_mlstm_in[0].astype(BF16)
    w_gate = jnp.pad(w_main[:, M_MAIN:], ((0, 0), (0, GATE_PAD - M_GATES)))
    b_gate = jnp.pad(b_mlstm_gate[0], (0, GATE_PAD - M_GATES)).reshape(1, GATE_PAD)
    p, gpre = _mlstm_inproj(xl, xc, row(g_mix_pre[0]), mod[0], w_main, w_gate)
    gc, gr = _gateprep(gpre, b_gate)
    hf, hb = _mlstm_scan(p, gc, gr)
    x_all = _mlstm_out(hf, hb, p, xl, xc, row(g_mlstm_head[0]), row(g_mix_post[0]), mod[0],
                       w_mlstm_out[0].astype(BF16))
    x_all = _ffn(x_all, row(g_ffn_pre[0]), row(g_ffn_post[0]), mod[0],
                 w_ffn_in[0].astype(BF16), w_ffn_out[0].astype(BF16))

    n_qk = (AH + AKV) * ADH
    w_qkv = w_attn_qkv[0].astype(BF16)
    w_qkv = jnp.concatenate([_half_split(w_qkv[:, :n_qk]), w_qkv[:, n_qk:]], axis=1)
    a_q, b_q = _rope_tables(g_attn_q[0], A_SCALE * LOG2E)
    a_k, b_k = _rope_tables(g_attn_k[0], 1.0)
    q, k, vt = _attn_qkv(x_all, row(g_mix_pre[1]), mod[1], a_q, b_q, a_k, b_k, w_qkv)
    o = _attention(q, k, vt)
    x_lat = _attn_out(o, x_all, row(g_mix_post[1]), mod[1], w_attn_out[0].astype(BF16))
    x_lat = _ffn(x_lat, row(g_ffn_pre[1]), row(g_ffn_post[1]), mod[1],
                 w_ffn_in[1].astype(BF16), w_ffn_out[1].astype(BF16))
    return x_lat.reshape(BATCH, SEQ, D)
```

```python
import functools
import math

import jax
import jax.numpy as jnp
from jax import lax
from jax.experimental import pallas as pl
from jax.experimental.pallas import tpu as pltpu

F32 = jnp.float32
BF16 = jnp.bfloat16

D = 2048
BATCH = 4
SEQ = 4096
CTX = 256
N_LAT = BATCH * SEQ
N_CTX = BATCH * CTX
N_ALL = N_LAT + N_CTX
N_MOD = 6
EPS = 1e-6
MOD_ROWS = 8

MH = 8
MDK = 128
MDV = 256
MQK = MH * MDK
MV = MH * MDV
M_MAIN = 2 * MQK + 2 * MV
M_GATES = 4 * MH
GATE_PAD = 128
SOFTCAP = 15.0
M_SCALE_LOG2 = -0.5 * math.log2(MDK)
LANES = 128

AH = 16
AKV = 4
ADH = 128
AG = AH // AKV
A_SCALE = ADH ** -0.5
LOG2E = math.log2(math.e)
VT_ONES = 16
VT_ROWS = ADH + VT_ONES
GRID_W = 64
ROPE_THETA = 10000.0

FFN = 5632

VMEM_LIMIT = 56 * 1024 * 1024
TM_PROJ = 512
TN_INPROJ = 2048
TN_QKV = 1024
TM_OUT = 256
TM_ATT_OUT = 2 * TM_OUT
TM_MLSTM_OUT = 2 * TM_OUT
TM_FFN = 1024
TF_FFN = 512
TN_MOD = 1024
TN_WPERM = 512
L_SCAN = 256
GATEPREP_CHUNKS = 4
TQ_ATT = 512
TK_ATT = 1024

_NT = (((1,), (1,)), ((), ()))
_TN = (((0,), (0,)), ((), ()))


def _params(n_axes):
    return pltpu.CompilerParams(
        dimension_semantics=("arbitrary",) * n_axes, vmem_limit_bytes=VMEM_LIMIT)


def _mod_row(i, tm):
    return jnp.where(i < N_LAT // tm, i // (SEQ // tm), BATCH)


def _mod_spec(tm, chunk):
    return pl.BlockSpec((None, 1, D), lambda i, j: (_mod_row(i, tm), 0, chunk))


def _adaln(x, gain, shift, scale):
    r = lax.rsqrt(jnp.mean(x * x, axis=-1, keepdims=True) + EPS)
    return (x * r) * (gain * (1.0 + scale)) + shift


def _post_norm_residual(x, y, gain, gate):
    r = lax.rsqrt(jnp.mean(y * y, axis=-1, keepdims=True) + EPS)
    return x + gate * ((y * r) * gain)


def _sigmoid(z):
    return 1.0 / (1.0 + jnp.exp(-z))


def _mod_kernel(c_ref, w_ref, b_ref, o_ref):
    c = c_ref[...]
    cond = (c * _sigmoid(c)).astype(BF16)
    o_ref[...] = jnp.dot(cond, w_ref[...].astype(BF16), preferred_element_type=F32) + b_ref[...]


def _modulation(c_all, w_mod, b_mod):
    depth = w_mod.shape[0]
    return pl.pallas_call(
        _mod_kernel,
        out_shape=jax.ShapeDtypeStruct((depth, MOD_ROWS, N_MOD * D), F32),
        grid=(depth, N_MOD * D // TN_MOD),
        in_specs=[
            pl.BlockSpec((MOD_ROWS, D), lambda l, j: (0, 0)),
            pl.BlockSpec((None, D, TN_MOD), lambda l, j: (l, 0, j)),
            pl.BlockSpec((None, 1, TN_MOD), lambda l, j: (l, 0, j)),
        ],
        out_specs=pl.BlockSpec((None, MOD_ROWS, TN_MOD), lambda l, j: (l, 0, j)),
        compiler_params=_params(2),
        name="modulation",
    )(c_all, w_mod, b_mod.reshape(depth, 1, N_MOD * D))


def _inproj_kernel(xl_ref, xc_ref, g_ref, sh_ref, sc_ref, w_ref, wg_ref, p_ref, gpre_ref, h_scr):
    i = pl.program_id(0)
    j = pl.program_id(1)

    @pl.when(j == 0)
    def _():
        is_latent = i < N_LAT // TM_PROJ
        half = TM_PROJ // 2
        for r in range(0, TM_PROJ, half):
            rows = slice(r, r + half)
            x = jnp.where(is_latent, xl_ref[rows, :], xc_ref[rows, :])
            h = _adaln(x, g_ref[...], sh_ref[...], sc_ref[...]).astype(BF16)
            h_scr[rows, :] = h
            gpre_ref[rows, :] = jnp.dot(h, wg_ref[...], preferred_element_type=F32)
            p_ref[rows, :] = jnp.dot(h, w_ref[...], preferred_element_type=F32).astype(BF16)

    @pl.when(j > 0)
    def _():
        p_ref[...] = jnp.dot(h_scr[...], w_ref[...], preferred_element_type=F32).astype(BF16)


def _two_source_specs(tm):
    n_lat = N_LAT // tm
    return [
        pl.BlockSpec((tm, D), lambda i, *_: (jnp.minimum(i, n_lat - 1), 0)),
        pl.BlockSpec((tm, D), lambda i, *_: (jnp.maximum(i - n_lat, 0), 0)),
    ]


def _mlstm_inproj(xl, xc, gain, mod, w_main, w_gate):
    tm, tn = TM_PROJ, TN_INPROJ
    vec = pl.BlockSpec((1, D), lambda i, j: (0, 0))
    return pl.pallas_call(
        _inproj_kernel,
        out_shape=(jax.ShapeDtypeStruct((N_ALL, M_MAIN), BF16),
                   jax.ShapeDtypeStruct((N_ALL, GATE_PAD), F32)),
        grid=(N_ALL // tm, M_MAIN // tn),
        in_specs=_two_source_specs(tm) + [
            vec, _mod_spec(tm, 0), _mod_spec(tm, 1),
            pl.BlockSpec((D, tn), lambda i, j: (0, j)),
            pl.BlockSpec((D, GATE_PAD), lambda i, j: (0, 0)),
        ],
        out_specs=(pl.BlockSpec((tm, tn), lambda i, j: (i, j)),
                   pl.BlockSpec((tm, GATE_PAD), lambda i, j: (i, 0))),
        scratch_shapes=[pltpu.VMEM((tm, D), BF16)],
        compiler_params=_params(2),
        name="mlstm_inproj",
    )(xl, xc, gain, mod, mod, w_main, w_gate)


def _gateprep_kernel(gpre_ref, b_ref, gc_ref, gr_ref):
    for c in range(GATEPREP_CHUNKS):
        rows = slice(c * L_SCAN, (c + 1) * L_SCAN)
        _gateprep_chunk(gpre_ref.at[rows, :], b_ref, gc_ref.at[rows, :], gr_ref.at[:, rows])


def _gateprep_chunk(gpre_ref, b_ref, gc_ref, gr_ref):
    L = gpre_ref.shape[0]
    z = gpre_ref[...] + b_ref[...]
    a = SOFTCAP * jnp.tanh(z * (1.0 / SOFTCAP))
    logsig = jnp.minimum(a, 0.0) - jnp.log(1.0 + jnp.exp(-jnp.abs(a)))
    row = lax.broadcasted_iota(jnp.int32, (L, GATE_PAD), 0)
    lane = lax.broadcasted_iota(jnp.int32, (L, GATE_PAD), 1)

    def scans(x, op, fill):
        pre, suf = x, x
        s = 1
        while s < L:
            pre = op(pre, jnp.where(row >= s, pltpu.roll(pre, s, 0), fill))
            suf = op(suf, jnp.where(row < L - s, pltpu.roll(suf, L - s, 0), fill))
            s *= 2
        return pre, suf

    fwd_lanes = lane < 2 * MH
    b_pre, b_suf = scans(logsig, jnp.add, 0.0)
    b = jnp.where(fwd_lanes, b_pre, b_suf)
    r = a - pltpu.roll(b, GATE_PAD - MH, 1)
    c_pre, c_suf = scans(r, jnp.maximum, -jnp.inf)
    cmax = jnp.where(fwd_lanes, c_pre, c_suf)
    is_gate_lane = (lane < MH) | ((lane >= 2 * MH) & (lane < 3 * MH))
    gc_ref[...] = jnp.where(is_gate_lane, cmax, b) * LOG2E
    gr_ref[...] = (r * LOG2E).T[:M_GATES, :]


def _gateprep(gpre, bias):
    L = GATEPREP_CHUNKS * L_SCAN
    return pl.pallas_call(
        _gateprep_kernel,
        out_shape=(jax.ShapeDtypeStruct((N_ALL, GATE_PAD), F32),
                   jax.ShapeDtypeStruct((M_GATES, N_ALL), F32)),
        grid=(N_ALL // L,),
        in_specs=[pl.BlockSpec((L, GATE_PAD), lambda i: (i, 0)),
                  pl.BlockSpec((1, GATE_PAD), lambda i: (0, 0))],
        out_specs=(pl.BlockSpec((L, GATE_PAD), lambda i: (i, 0)),
                   pl.BlockSpec((M_GATES, L), lambda i: (0, i))),
        compiler_params=_params(1),
        name="mlstm_gateprep",
    )(gpre, bias)


def _scan_unit(q, k, vx, r_row, cmax_col, b_col, mask, end, cx_ref, m_ref, h_out):
    L = q.shape[0]
    m = m_ref[0:1, 0:1]
    u = jnp.maximum(m, cmax_col)
    u_b = jnp.broadcast_to(u - M_SCALE_LOG2, (L, L))
    dmat = jnp.where(mask, jnp.exp2(r_row - u_b), 0.0)
    sm = (lax.dot_general(q, k, _NT, preferred_element_type=F32) * dmat).astype(BF16)
    cx = cx_ref[...]
    intra = jnp.dot(sm, vx, preferred_element_type=F32)
    inter = jnp.dot(q, cx.astype(BF16), preferred_element_type=F32)
    u_rep = u_b[:, :LANES]
    w_inter = jnp.exp2(m - u_rep)
    den = intra[:, MDV:] + w_inter * inter[:, MDV:]
    floor = jnp.exp2(-(jnp.broadcast_to(b_col, (L, LANES)) + u_rep + M_SCALE_LOG2))
    inv = 1.0 / jnp.maximum(jnp.abs(den), floor)
    for t in range(MDV // LANES):
        cols = slice(t * LANES, (t + 1) * LANES)
        h_out[:, cols] = ((intra[:, cols] + w_inter * inter[:, cols]) * inv).astype(h_out.dtype)

    u_end = u[end:end + 1, :]
    w_row = jnp.exp2(r_row - u_end)
    kw = (k.astype(F32).T * w_row).astype(BF16)
    cx_ref[...] = jnp.exp2(m - u_end) * cx + jnp.dot(kw, vx, preferred_element_type=F32)
    m_ref[...] = jnp.broadcast_to(b_col[end:end + 1, :] + u_end, m_ref.shape)


def _scan_kernel(qf, kf, vf, gcf, grf, qb, kb, vb, gcb, grb, hf_ref, hb_ref, cx_scr, m_scr):
    L = qf.shape[0]

    @pl.when(pl.program_id(1) == 0)
    def _():
        cx_scr[...] = jnp.zeros_like(cx_scr)
        m_scr[...] = jnp.zeros_like(m_scr)

    row = lax.broadcasted_iota(jnp.int32, (L, L), 0)
    col = lax.broadcasted_iota(jnp.int32, (L, L), 1)
    ones = jnp.ones((L, LANES), BF16)
    directions = (
        (qf, kf, vf, gcf, grf, hf_ref, col <= row, 0, L - 1),
        (qb, kb, vb, gcb, grb, hb_ref, col >= row, 2 * MH, 0),
    )
    for d, (q_ref, k_ref, v_ref, gc_ref, gr_ref, h_ref, mask, goff, end) in enumerate(directions):
        for h in range(MH):
            ig, fg = goff + h, goff + MH + h
            _scan_unit(
                q_ref[:, h * MDK:(h + 1) * MDK], k_ref[:, h * MDK:(h + 1) * MDK],
                jnp.concatenate([v_ref[:, h * MDV:(h + 1) * MDV], ones], axis=1),
                gr_ref[ig:ig + 1, :], gc_ref[:, ig:ig + 1], gc_ref[:, fg:fg + 1],
                mask, end, cx_scr.at[d * MH + h], m_scr.at[d * MH + h],
                h_ref.at[:, h * MDV:(h + 1) * MDV])


def _mlstm_scan(p, gc, gr):
    L = L_SCAN
    n_ctx_chunks = CTX // L
    n_lat_chunks = SEQ // L
    steps = n_ctx_chunks + n_lat_chunks
    ctx0 = N_LAT // L

    def fwd(b, s):
        return jnp.where(s < n_ctx_chunks, ctx0 + b * n_ctx_chunks + s,
                         b * n_lat_chunks + s - n_ctx_chunks)

    def bwd(b, s):
        return jnp.where(s < n_ctx_chunks, ctx0 + b * n_ctx_chunks + (n_ctx_chunks - 1 - s),
                         b * n_lat_chunks + (n_lat_chunks - 1 - (s - n_ctx_chunks)))

    def specs(idx):
        return [
            pl.BlockSpec((L, MQK), lambda b, s: (idx(b, s), 0)),
            pl.BlockSpec((L, MQK), lambda b, s: (idx(b, s), 1)),
            pl.BlockSpec((L, MV), lambda b, s: (idx(b, s), 1)),
            pl.BlockSpec((L, GATE_PAD), lambda b, s: (idx(b, s), 0)),
            pl.BlockSpec((M_GATES, L), lambda b, s: (0, idx(b, s))),
        ]

    return pl.pallas_call(
        _scan_kernel,
        out_shape=(jax.ShapeDtypeStruct((N_ALL, MV), BF16),
                   jax.ShapeDtypeStruct((N_ALL, MV), BF16)),
        grid=(BATCH, steps),
        in_specs=specs(fwd) + specs(bwd),
        out_specs=(pl.BlockSpec((L, MV), lambda b, s: (fwd(b, s), 0)),
                   pl.BlockSpec((L, MV), lambda b, s: (bwd(b, s), 0))),
        scratch_shapes=[pltpu.VMEM((2 * MH, MDK, MDV + LANES), F32),
                        pltpu.VMEM((2 * MH, 8, LANES), F32)],
        compiler_params=_params(2),
        name="mlstm_scan",
    )(p, p, p, gc, gr, p, p, p, gc, gr)


def _mlstm_out_kernel(hf_ref, hb_ref, o_ref, xl_ref, xc_ref, gh_ref, gp_ref, gt_ref, w_ref,
                      out_ref, a_scr):
    is_latent = pl.program_id(0) < N_LAT // TM_MLSTM_OUT
    for r0 in range(0, TM_MLSTM_OUT, TM_OUT):
        rows = slice(r0, r0 + TM_OUT)
        for h in range(MH):
            sl = slice(h * MDV, (h + 1) * MDV)
            hh = hf_ref[rows, sl].astype(F32) + hb_ref[rows, sl].astype(F32)
            r = lax.rsqrt(jnp.mean(hh * hh, axis=-1, keepdims=True) + EPS)
            hn = (hh * r) * gh_ref[:, sl]
            a_scr[rows, sl] = (_sigmoid(o_ref[rows, sl].astype(F32)) * hn).astype(BF16)
        y = jnp.dot(a_scr[rows, :], w_ref[...], preferred_element_type=F32)
        x = jnp.where(is_latent, xl_ref[rows, :], xc_ref[rows, :])
        out_ref[rows, :] = _post_norm_residual(x, y, gp_ref[...], gt_ref[...])


def _mlstm_out(hf, hb, p, xl, xc, g_head, g_post, mod, w_out):
    tm = TM_MLSTM_OUT
    vec = pl.BlockSpec((1, D), lambda i: (0, 0))
    return pl.pallas_call(
        _mlstm_out_kernel,
        out_shape=jax.ShapeDtypeStruct((N_ALL, D), F32),
        grid=(N_ALL // tm,),
        in_specs=[
            pl.BlockSpec((tm, MV), lambda i: (i, 0)),
            pl.BlockSpec((tm, MV), lambda i: (i, 0)),
            pl.BlockSpec((tm, MV), lambda i: (i, 2)),
        ] + _two_source_specs(tm) + [
            vec, vec,
            pl.BlockSpec((None, 1, D), lambda i: (_mod_row(i, tm), 0, 2)),
            pl.BlockSpec((MV, D), lambda i: (0, 0), pipeline_mode=pl.Buffered(1)),
        ],
        out_specs=pl.BlockSpec((tm, D), lambda i: (i, 0)),
        scratch_shapes=[pltpu.VMEM((tm, MV), BF16)],
        compiler_params=_params(1),
        name="mlstm_out",
    )(hf, hb, p, xl, xc, g_head, g_post, mod, w_out)


def _ffn_kernel(x_ref, g_ref, sh_ref, sc_ref, gt_ref, gp_ref, wg_ref, wu_ref, wo_ref, out_ref,
                h_scr):
    f = pl.program_id(1)
    last_f = pl.num_programs(1) - 1
    half = TM_FFN // 2

    def step(first, last):
        for r in range(0, TM_FFN, half):
            rows = slice(r, r + half)
            if first:
                h_scr[rows, :] = _adaln(x_ref[rows, :], g_ref[...], sh_ref[...],
                                        sc_ref[...]).astype(BF16)
            h = h_scr[rows, :]
            gate = jnp.dot(h, wg_ref[...], preferred_element_type=F32)
            up = jnp.dot(h, wu_ref[...], preferred_element_type=F32)
            act = ((gate * _sigmoid(gate)) * up).astype(BF16)
            y = jnp.dot(act, wo_ref[...], preferred_element_type=F32)
            if not first:
                y = out_ref[rows, :] + y
            if last:
                y = _post_norm_residual(x_ref[rows, :], y, gp_ref[...], gt_ref[...])
            out_ref[rows, :] = y

    pl.when(f == 0)(functools.partial(step, True, False))
    pl.when((f > 0) & (f < last_f))(functools.partial(step, False, False))
    pl.when(f == last_f)(functools.partial(step, False, True))


def _ffn(x_rows, g_pre, g_post, mod, w_in, w_out, layer):
    n_rows = x_rows.shape[0]
    tm, tf = TM_FFN, TF_FFN
    nf = FFN // tf
    assert nf >= 2
    vec = pl.BlockSpec((1, D), lambda i, f: (0, 0))
    return pl.pallas_call(
        _ffn_kernel,
        out_shape=jax.ShapeDtypeStruct((n_rows, D), F32),
        grid=(n_rows // tm, nf),
        in_specs=[
            pl.BlockSpec((tm, D), lambda i, f: (i, 0)),
            vec, _mod_spec(tm, 3), _mod_spec(tm, 4), _mod_spec(tm, 5), vec,
            pl.BlockSpec((None, D, tf), lambda i, f: (layer, 0, f)),
            pl.BlockSpec((None, D, tf), lambda i, f: (layer, 0, nf + f)),
            pl.BlockSpec((None, tf, D), lambda i, f: (layer, f, 0)),
        ],
        out_specs=pl.BlockSpec((tm, D), lambda i, f: (i, 0)),
        scratch_shapes=[pltpu.VMEM((tm, D), BF16)],
        compiler_params=_params(2),
        name="ffn",
    )(x_rows, g_pre, mod, mod, mod, g_post, w_in, w_in, w_out)


def _norm_rope_pair(h_bf16, w_ref, cols, a, b):
    y = jnp.dot(h_bf16, w_ref[:, cols], preferred_element_type=F32)
    outs = []
    for h in range(2):
        yy = y[:, h * ADH:(h + 1) * ADH]
        r = lax.rsqrt(jnp.mean(yy * yy, axis=-1, keepdims=True) + EPS)
        outs.append(((yy * a + pltpu.roll(yy, ADH // 2, 1) * b) * r).astype(BF16))
    return jnp.concatenate(outs, axis=1)


def _qkv_kernel(x_ref, g_ref, sh_ref, sc_ref, aq_ref, bq_ref, ak_ref, bk_ref, w_ref,
                q_ref, k_ref, vt_ref, h_scr):
    j = pl.program_id(1)
    n_q = AH * ADH // TN_QKV
    kv_cols = AKV * ADH
    pair = 2 * ADH

    @pl.when(j == 0)
    def _():
        h_scr[...] = _adaln(x_ref[...], g_ref[...], sh_ref[...], sc_ref[...]).astype(BF16)

    @pl.when(j < n_q)
    def _():
        h = h_scr[...]
        for c in range(0, TN_QKV, pair):
            q_ref[:, c:c + pair] = _norm_rope_pair(h, w_ref, slice(c, c + pair),
                                                   aq_ref[...], bq_ref[...])

    @pl.when(j == n_q)
    def _():
        h = h_scr[...]
        for c in range(0, kv_cols, pair):
            k_ref[:, c:c + pair] = _norm_rope_pair(h, w_ref, slice(c, c + pair),
                                                   ak_ref[...], bk_ref[...])
        v = jnp.dot(h, w_ref[:, kv_cols:], preferred_element_type=F32)
        vt = v.T
        ones = jnp.ones((VT_ONES, vt.shape[1]), BF16)
        for hd in range(AKV):
            vt_ref[hd * VT_ROWS:hd * VT_ROWS + ADH, :] = vt[hd * ADH:(hd + 1) * ADH, :].astype(BF16)
            vt_ref[hd * VT_ROWS + ADH:(hd + 1) * VT_ROWS, :] = ones


def _attn_qkv(x_all, gain, mod, a_q, b_q, a_k, b_k, w_qkv):
    tm, tn = TM_PROJ, TN_QKV
    n_q = AH * ADH // tn
    n_lat = N_LAT // tm
    per_seq = SEQ // tm
    kv_cols = AKV * ADH
    assert tn == 2 * kv_cols
    vec = pl.BlockSpec((1, D), lambda i, j: (0, 0))
    rope = pl.BlockSpec((tm, ADH), lambda i, j: (jnp.where(i < n_lat, i % per_seq, per_seq), 0))
    return pl.pallas_call(
        _qkv_kernel,
        out_shape=(jax.ShapeDtypeStruct((N_ALL, AH * ADH), BF16),
                   jax.ShapeDtypeStruct((N_ALL, kv_cols), BF16),
                   jax.ShapeDtypeStruct((AKV * VT_ROWS, N_ALL), BF16)),
        grid=(N_ALL // tm, n_q + 1),
        in_specs=[
            pl.BlockSpec((tm, D), lambda i, j: (i, 0)),
            vec, _mod_spec(tm, 0), _mod_spec(tm, 1), rope, rope, rope, rope,
            pl.BlockSpec((D, tn), lambda i, j: (0, j)),
        ],
        out_specs=(pl.BlockSpec((tm, tn), lambda i, j: (i, jnp.minimum(j, n_q - 1))),
                   pl.BlockSpec((tm, kv_cols), lambda i, j: (i, 0)),
                   pl.BlockSpec((AKV * VT_ROWS, tm), lambda i, j: (0, i))),
        scratch_shapes=[pltpu.VMEM((tm, D), BF16)],
        compiler_params=_params(2),
        name="attn_qkv",
    )(x_all, gain, mod, mod, a_q, b_q, a_k, b_k, w_qkv)


def _attn_kernel(q_ref, kl_ref, kc_ref, vtl_ref, vtc_ref, o_ref):
    tq = q_ref.shape[0]
    nq = AG * tq
    q = jnp.concatenate([q_ref[:, g * ADH:(g + 1) * ADH] for g in range(AG)], axis=0)

    def block(k, vt, m, acc):
        st = lax.dot_general(k, q, _NT, preferred_element_type=F32)
        m_new = jnp.maximum(m, jnp.max(st, axis=0, keepdims=True))
        p = jnp.exp2(st - m_new).astype(BF16)
        acc = jnp.exp2(m - m_new) * acc + jnp.dot(vt, p, preferred_element_type=F32)
        return m_new, acc

    m = jnp.full((1, nq), -jnp.inf, F32)
    acc = jnp.zeros((VT_ROWS, nq), F32)
    for c in range(SEQ // TK_ATT):
        m, acc = block(kl_ref[c * TK_ATT:(c + 1) * TK_ATT, :],
                       vtl_ref[:, c * TK_ATT:(c + 1) * TK_ATT], m, acc)
    m, acc = block(kc_ref[...], vtc_ref[...], m, acc)
    out = (acc[:ADH, :] * (1.0 / acc[ADH:ADH + 1, :])).T
    for g in range(AG):
        o_ref[:, g * ADH:(g + 1) * ADH] = out[g * tq:(g + 1) * tq, :].astype(BF16)


def _attention(q, k, vt):
    tq = TQ_ATT
    per_seq = SEQ // tq
    return pl.pallas_call(
        _attn_kernel,
        out_shape=jax.ShapeDtypeStruct((N_LAT, AH * ADH), BF16),
        grid=(BATCH, AKV, per_seq),
        in_specs=[
            pl.BlockSpec((tq, AG * ADH), lambda b, h, i: (b * per_seq + i, h)),
            pl.BlockSpec((SEQ, ADH), lambda b, h, i: (b, h)),
            pl.BlockSpec((CTX, ADH), lambda b, h, i: (N_LAT // CTX + b, h)),
            pl.BlockSpec((VT_ROWS, SEQ), lambda b, h, i: (h, b)),
            pl.BlockSpec((VT_ROWS, CTX), lambda b, h, i: (h, N_LAT // CTX + b)),
        ],
        out_specs=pl.BlockSpec((tq, AG * ADH), lambda b, h, i: (b * per_seq + i, h)),
        compiler_params=_params(3),
        name="attention",
    )(q, k, k, vt, vt)


def _attn_out_kernel(o_ref, x_ref, gp_ref, gt_ref, w_ref, out_ref):
    for r in range(0, TM_ATT_OUT, TM_OUT):
        rows = slice(r, r + TM_OUT)
        y = jnp.dot(o_ref[rows, :], w_ref[...], preferred_element_type=F32)
        out_ref[rows, :] = _post_norm_residual(x_ref[rows, :], y, gp_ref[...], gt_ref[...])


def _attn_out(o, x_all, g_post, mod, w_out):
    tm = TM_ATT_OUT
    return pl.pallas_call(
        _attn_out_kernel,
        out_shape=jax.ShapeDtypeStruct((N_LAT, D), F32),
        grid=(N_LAT // tm,),
        in_specs=[
            pl.BlockSpec((tm, D), lambda i: (i, 0)),
            pl.BlockSpec((tm, D), lambda i: (i, 0)),
            pl.BlockSpec((1, D), lambda i: (0, 0)),
            pl.BlockSpec((None, 1, D), lambda i: (_mod_row(i, tm), 0, 2)),
            pl.BlockSpec((D, D), lambda i: (0, 0)),
        ],
        out_specs=pl.BlockSpec((tm, D), lambda i: (i, 0)),
        compiler_params=_params(1),
        name="attn_out",
    )(o, x_all, g_post, mod, w_out)


def _half_split(a):
    lead = a.shape[:-1]
    n = a.shape[-1] // ADH
    a = a.reshape(*lead, n, ADH // 2, 2)
    return jnp.swapaxes(a, -1, -2).reshape(*lead, n * ADH)


def _qkv_weight_kernel(w_ref, o_ref):
    first_head = pl.program_id(0) * (TN_WPERM // ADH)
    src = lax.broadcasted_iota(jnp.int32, (ADH, ADH), 0)
    dst = lax.broadcasted_iota(jnp.int32, (ADH, ADH), 1)
    half = ADH // 2
    split_src = jnp.where(dst < half, 2 * dst, 2 * (dst - half) + 1)
    for h in range(TN_WPERM // ADH):
        is_qk = first_head + h < AH + AKV
        perm = (src == jnp.where(is_qk, split_src, dst)).astype(BF16)
        cols = slice(h * ADH, (h + 1) * ADH)
        o_ref[:, cols] = jnp.dot(w_ref[:, cols].astype(BF16), perm,
                                 preferred_element_type=F32).astype(BF16)


def _qkv_weight(w_qkv):
    n_cols = w_qkv.shape[1]
    return pl.pallas_call(
        _qkv_weight_kernel,
        out_shape=jax.ShapeDtypeStruct((D, n_cols), BF16),
        grid=(n_cols // TN_WPERM,),
        in_specs=[pl.BlockSpec((D, TN_WPERM), lambda j: (0, j))],
        out_specs=pl.BlockSpec((D, TN_WPERM), lambda j: (0, j)),
        compiler_params=_params(1),
        name="qkv_weight",
    )(w_qkv)


def _rope_tables(gain, scale):
    rows = SEQ // GRID_W
    t_row = jnp.repeat(jnp.arange(rows, dtype=F32), GRID_W)
    t_col = jnp.tile(jnp.arange(GRID_W, dtype=F32), rows)
    per_axis = ADH // 2
    inv = ROPE_THETA ** (-jnp.arange(0, per_axis, 2, dtype=F32) / per_axis)
    ang = jnp.concatenate([t_row[:, None] * inv, t_col[:, None] * inv], axis=-1)
    ang = jnp.concatenate([ang, jnp.zeros((TM_PROJ, per_axis), F32)], axis=0)
    cos = jnp.concatenate([jnp.cos(ang), jnp.cos(ang)], axis=-1)
    sin = jnp.concatenate([-jnp.sin(ang), jnp.sin(ang)], axis=-1)
    g = _half_split(gain.astype(F32))
    return (g * scale) * cos, (jnp.roll(g, per_axis) * scale) * sin


def kernel(x, c, ctx, c_ctx, w_mod, b_mod, g_mix_pre, g_mix_post, g_ffn_pre, g_ffn_post,
           w_mlstm_in, b_mlstm_gate, g_mlstm_head, w_mlstm_out,
           w_attn_qkv, g_attn_q, g_attn_k, w_attn_out, w_ffn_in, w_ffn_out):
    assert x.shape == (BATCH, SEQ, D) and ctx.shape == (BATCH, CTX, D)
    xl = x.reshape(N_LAT, D)
    xc = ctx.reshape(N_CTX, D)
    row = lambda a: a.reshape(1, -1)

    c_all = jnp.concatenate([c, c_ctx[None, :], jnp.zeros((MOD_ROWS - BATCH - 1, D), F32)], axis=0)
    mod = _modulation(c_all, w_mod, b_mod).reshape(2, MOD_ROWS, 1, N_MOD * D)

    w_main = w_mlstm_in[0].astype(BF16)
    w_gate = jnp.pad(w_main[:, M_MAIN:], ((0, 0), (0, GATE_PAD - M_GATES)))
    b_gate = jnp.pad(b_mlstm_gate[0], (0, GATE_PAD - M_GATES)).reshape(1, GATE_PAD)
    p, gpre = _mlstm_inproj(xl, xc, row(g_mix_pre[0]), mod[0], w_main, w_gate)
    gc, gr = _gateprep(gpre, b_gate)
    hf, hb = _mlstm_scan(p, gc, gr)
    x_all = _mlstm_out(hf, hb, p, xl, xc, row(g_mlstm_head[0]), row(g_mix_post[0]), mod[0],
                       w_mlstm_out[0].astype(BF16))
    w_ffn_in16 = w_ffn_in.astype(BF16)
    w_ffn_out16 = w_ffn_out.astype(BF16)
    x_all = _ffn(x_all, row(g_ffn_pre[0]), row(g_ffn_post[0]), mod[0], w_ffn_in16, w_ffn_out16, 0)

    w_qkv = _qkv_weight(w_attn_qkv[0])
    a_q, b_q = _rope_tables(g_attn_q[0], A_SCALE * LOG2E)
    a_k, b_k = _rope_tables(g_attn_k[0], 1.0)
    q, k, vt = _attn_qkv(x_all, row(g_mix_pre[1]), mod[1], a_q, b_q, a_k, b_k, w_qkv)
    o = _attention(q, k, vt)
    x_lat = _attn_out(o, x_all, row(g_mix_post[1]), mod[1], w_attn_out[0].astype(BF16))
    x_lat = _ffn(x_lat, row(g_ffn_pre[1]), row(g_ffn_post[1]), mod[1], w_ffn_in16, w_ffn_out16, 1)
    return x_lat.reshape(BATCH, SEQ, D)
```

```python
import functools
import math

import jax
import jax.numpy as jnp
from jax import lax
from jax.experimental import pallas as pl
from jax.experimental.pallas import tpu as pltpu

F32 = jnp.float32
BF16 = jnp.bfloat16

D = 2048
BATCH = 4
SEQ = 4096
CTX = 256
N_LAT = BATCH * SEQ
N_CTX = BATCH * CTX
N_ALL = N_LAT + N_CTX
N_MOD = 6
EPS = 1e-6
MOD_ROWS = 8

MH = 8
MDK = 128
MDV = 256
MQK = MH * MDK
MV = MH * MDV
M_MAIN = 2 * MQK + 2 * MV
M_GATES = 4 * MH
GATE_PAD = 128
SOFTCAP = 15.0
M_SCALE_LOG2 = -0.5 * math.log2(MDK)
LANES = 128

AH = 16
AKV = 4
ADH = 128
AG = AH // AKV
A_SCALE = ADH ** -0.5
LOG2E = math.log2(math.e)
VT_ONES = 16
VT_ROWS = ADH + VT_ONES
GRID_W = 64
ROPE_THETA = 10000.0

FFN = 5632

VMEM_LIMIT = 56 * 1024 * 1024
TM_PROJ = 512
TN_INPROJ = 2048
TN_QKV = 1024
TM_OUT = 256
TM_ATT_OUT = 2 * TM_OUT
TM_MLSTM_OUT = 2 * TM_OUT
TM_FFN = 1024
TF_FFN = 512
TN_MOD = 1024
TN_WPERM = 512
L_SCAN = 256
GATEPREP_CHUNKS = 4
TQ_ATT = 512
TK_ATT = 1024

_NT = (((1,), (1,)), ((), ()))
_TN = (((0,), (0,)), ((), ()))


def _params(n_axes):
    return pltpu.CompilerParams(
        dimension_semantics=("arbitrary",) * n_axes, vmem_limit_bytes=VMEM_LIMIT)


def _mod_row(i, tm):
    return jnp.where(i < N_LAT // tm, i // (SEQ // tm), BATCH)


def _mod_spec(tm, chunk):
    return pl.BlockSpec((None, 1, D), lambda i, j: (_mod_row(i, tm), 0, chunk))


def _adaln(x, gain, shift, scale):
    r = lax.rsqrt(jnp.mean(x * x, axis=-1, keepdims=True) + EPS)
    return (x * r) * (gain * (1.0 + scale)) + shift


def _post_norm_residual(x, y, gain, gate):
    r = lax.rsqrt(jnp.mean(y * y, axis=-1, keepdims=True) + EPS)
    return x + gate * ((y * r) * gain)


def _sigmoid(z):
    return 1.0 / (1.0 + jnp.exp2(z * (-LOG2E)))


def _mod_kernel(c_ref, w_ref, b_ref, o_ref):
    c = c_ref[...]
    cond = (c * _sigmoid(c)).astype(BF16)
    o_ref[...] = jnp.dot(cond, w_ref[...].astype(BF16), preferred_element_type=F32) + b_ref[...]


def _modulation(c_all, w_mod, b_mod):
    depth = w_mod.shape[0]
    return pl.pallas_call(
        _mod_kernel,
        out_shape=jax.ShapeDtypeStruct((depth, MOD_ROWS, N_MOD * D), F32),
        grid=(depth, N_MOD * D // TN_MOD),
        in_specs=[
            pl.BlockSpec((MOD_ROWS, D), lambda l, j: (0, 0)),
            pl.BlockSpec((None, D, TN_MOD), lambda l, j: (l, 0, j)),
            pl.BlockSpec((None, 1, TN_MOD), lambda l, j: (l, 0, j)),
        ],
        out_specs=pl.BlockSpec((None, MOD_ROWS, TN_MOD), lambda l, j: (l, 0, j)),
        compiler_params=_params(2),
        name="modulation",
    )(c_all, w_mod, b_mod.reshape(depth, 1, N_MOD * D))


def _inproj_kernel(xl_ref, xc_ref, g_ref, sh_ref, sc_ref, w_ref, wg_ref, p_ref, gpre_ref, h_scr):
    i = pl.program_id(0)
    j = pl.program_id(1)

    @pl.when(j == 0)
    def _():
        is_latent = i < N_LAT // TM_PROJ
        half = TM_PROJ // 2
        for r in range(0, TM_PROJ, half):
            rows = slice(r, r + half)
            x = jnp.where(is_latent, xl_ref[rows, :], xc_ref[rows, :])
            h = _adaln(x, g_ref[...], sh_ref[...], sc_ref[...]).astype(BF16)
            h_scr[rows, :] = h
            gpre_ref[rows, :] = jnp.dot(h, wg_ref[...], preferred_element_type=F32)
            p_ref[rows, :] = jnp.dot(h, w_ref[...], preferred_element_type=F32).astype(BF16)

    @pl.when(j > 0)
    def _():
        p_ref[...] = jnp.dot(h_scr[...], w_ref[...], preferred_element_type=F32).astype(BF16)


def _two_source_specs(tm):
    n_lat = N_LAT // tm
    return [
        pl.BlockSpec((tm, D), lambda i, *_: (jnp.minimum(i, n_lat - 1), 0)),
        pl.BlockSpec((tm, D), lambda i, *_: (jnp.maximum(i - n_lat, 0), 0)),
    ]


def _mlstm_inproj(xl, xc, gain, mod, w_main, w_gate):
    tm, tn = TM_PROJ, TN_INPROJ
    vec = pl.BlockSpec((1, D), lambda i, j: (0, 0))
    return pl.pallas_call(
        _inproj_kernel,
        out_shape=(jax.ShapeDtypeStruct((N_ALL, M_MAIN), BF16),
                   jax.ShapeDtypeStruct((N_ALL, GATE_PAD), F32)),
        grid=(N_ALL // tm, M_MAIN // tn),
        in_specs=_two_source_specs(tm) + [
            vec, _mod_spec(tm, 0), _mod_spec(tm, 1),
            pl.BlockSpec((D, tn), lambda i, j: (0, j)),
            pl.BlockSpec((D, GATE_PAD), lambda i, j: (0, 0)),
        ],
        out_specs=(pl.BlockSpec((tm, tn), lambda i, j: (i, j)),
                   pl.BlockSpec((tm, GATE_PAD), lambda i, j: (i, 0))),
        scratch_shapes=[pltpu.VMEM((tm, D), BF16)],
        compiler_params=_params(2),
        name="mlstm_inproj",
    )(xl, xc, gain, mod, mod, w_main, w_gate)


def _gateprep_kernel(gpre_ref, b_ref, gc_ref, gr_ref):
    for c in range(GATEPREP_CHUNKS):
        rows = slice(c * L_SCAN, (c + 1) * L_SCAN)
        _gateprep_chunk(gpre_ref.at[rows, :], b_ref, gc_ref.at[rows, :], gr_ref.at[:, rows])


def _gateprep_chunk(gpre_ref, b_ref, gc_ref, gr_ref):
    L = gpre_ref.shape[0]
    z = gpre_ref[...] + b_ref[...]
    a = SOFTCAP * jnp.tanh(z * (1.0 / SOFTCAP))
    logsig = jnp.minimum(a, 0.0) - jnp.log(1.0 + jnp.exp(-jnp.abs(a)))
    row = lax.broadcasted_iota(jnp.int32, (L, GATE_PAD), 0)
    lane = lax.broadcasted_iota(jnp.int32, (L, GATE_PAD), 1)

    def scans(x, op, fill):
        pre, suf = x, x
        s = 1
        while s < L:
            pre = op(pre, jnp.where(row >= s, pltpu.roll(pre, s, 0), fill))
            suf = op(suf, jnp.where(row < L - s, pltpu.roll(suf, L - s, 0), fill))
            s *= 2
        return pre, suf

    fwd_lanes = lane < 2 * MH
    b_pre, b_suf = scans(logsig, jnp.add, 0.0)
    b = jnp.where(fwd_lanes, b_pre, b_suf)
    r = a - pltpu.roll(b, GATE_PAD - MH, 1)
    c_pre, c_suf = scans(r, jnp.maximum, -jnp.inf)
    cmax = jnp.where(fwd_lanes, c_pre, c_suf)
    is_gate_lane = (lane < MH) | ((lane >= 2 * MH) & (lane < 3 * MH))
    gc_ref[...] = jnp.where(is_gate_lane, cmax, b) * LOG2E
    gr_ref[...] = (r * LOG2E).T[:M_GATES, :]


def _gateprep(gpre, bias):
    L = GATEPREP_CHUNKS * L_SCAN
    return pl.pallas_call(
        _gateprep_kernel,
        out_shape=(jax.ShapeDtypeStruct((N_ALL, GATE_PAD), F32),
                   jax.ShapeDtypeStruct((M_GATES, N_ALL), F32)),
        grid=(N_ALL // L,),
        in_specs=[pl.BlockSpec((L, GATE_PAD), lambda i: (i, 0)),
                  pl.BlockSpec((1, GATE_PAD), lambda i: (0, 0))],
        out_specs=(pl.BlockSpec((L, GATE_PAD), lambda i: (i, 0)),
                   pl.BlockSpec((M_GATES, L), lambda i: (0, i))),
        compiler_params=_params(1),
        name="mlstm_gateprep",
    )(gpre, bias)


def _scan_unit(q, k, vx, r_row, cmax_col, b_col, mask, end, cx_ref, m_ref, h_out):
    L = q.shape[0]
    m = m_ref[0:1, 0:1]
    u = jnp.maximum(m, cmax_col)
    u_b = jnp.broadcast_to(u - M_SCALE_LOG2, (L, L))
    dmat = jnp.where(mask, jnp.exp2(r_row - u_b), 0.0)
    sm = (lax.dot_general(q, k, _NT, preferred_element_type=F32) * dmat).astype(BF16)
    cx = cx_ref[...]
    intra = jnp.dot(sm, vx, preferred_element_type=F32)
    inter = jnp.dot(q, cx.astype(BF16), preferred_element_type=F32)
    u_rep = u_b[:, :LANES]
    w_inter = jnp.exp2(m - u_rep)
    den = intra[:, MDV:] + w_inter * inter[:, MDV:]
    floor = jnp.exp2(-(jnp.broadcast_to(b_col, (L, LANES)) + u_rep + M_SCALE_LOG2))
    inv = 1.0 / jnp.maximum(jnp.abs(den), floor)
    for t in range(MDV // LANES):
        cols = slice(t * LANES, (t + 1) * LANES)
        h_out[:, cols] = ((intra[:, cols] + w_inter * inter[:, cols]) * inv).astype(h_out.dtype)

    u_end = u[end:end + 1, :]
    w_row = jnp.exp2(r_row - u_end)
    kw = (k.astype(F32).T * w_row).astype(BF16)
    cx_ref[...] = jnp.exp2(m - u_end) * cx + jnp.dot(kw, vx, preferred_element_type=F32)
    m_ref[...] = jnp.broadcast_to(b_col[end:end + 1, :] + u_end, m_ref.shape)


def _scan_kernel(qf, kf, vf, gcf, grf, qb, kb, vb, gcb, grb, hf_ref, hb_ref, cx_scr, m_scr):
    L = qf.shape[0]

    @pl.when(pl.program_id(1) == 0)
    def _():
        cx_scr[...] = jnp.zeros_like(cx_scr)
        m_scr[...] = jnp.zeros_like(m_scr)

    row = lax.broadcasted_iota(jnp.int32, (L, L), 0)
    col = lax.broadcasted_iota(jnp.int32, (L, L), 1)
    ones = jnp.ones((L, LANES), BF16)
    directions = (
        (qf, kf, vf, gcf, grf, hf_ref, col <= row, 0, L - 1),
        (qb, kb, vb, gcb, grb, hb_ref, col >= row, 2 * MH, 0),
    )
    for d, (q_ref, k_ref, v_ref, gc_ref, gr_ref, h_ref, mask, goff, end) in enumerate(directions):
        for h in range(MH):
            ig, fg = goff + h, goff + MH + h
            _scan_unit(
                q_ref[:, h * MDK:(h + 1) * MDK], k_ref[:, h * MDK:(h + 1) * MDK],
                jnp.concatenate([v_ref[:, h * MDV:(h + 1) * MDV], ones], axis=1),
                gr_ref[ig:ig + 1, :], gc_ref[:, ig:ig + 1], gc_ref[:, fg:fg + 1],
                mask, end, cx_scr.at[d * MH + h], m_scr.at[d * MH + h],
                h_ref.at[:, h * MDV:(h + 1) * MDV])


def _mlstm_scan(p, gc, gr):
    L = L_SCAN
    n_ctx_chunks = CTX // L
    n_lat_chunks = SEQ // L
    steps = n_ctx_chunks + n_lat_chunks
    ctx0 = N_LAT // L

    def fwd(b, s):
        return jnp.where(s < n_ctx_chunks, ctx0 + b * n_ctx_chunks + s,
                         b * n_lat_chunks + s - n_ctx_chunks)

    def bwd(b, s):
        return jnp.where(s < n_ctx_chunks, ctx0 + b * n_ctx_chunks + (n_ctx_chunks - 1 - s),
                         b * n_lat_chunks + (n_lat_chunks - 1 - (s - n_ctx_chunks)))

    def specs(idx):
        return [
            pl.BlockSpec((L, MQK), lambda b, s: (idx(b, s), 0)),
            pl.BlockSpec((L, MQK), lambda b, s: (idx(b, s), 1)),
            pl.BlockSpec((L, MV), lambda b, s: (idx(b, s), 1)),
            pl.BlockSpec((L, GATE_PAD), lambda b, s: (idx(b, s), 0)),
            pl.BlockSpec((M_GATES, L), lambda b, s: (0, idx(b, s))),
        ]

    return pl.pallas_call(
        _scan_kernel,
        out_shape=(jax.ShapeDtypeStruct((N_ALL, MV), BF16),
                   jax.ShapeDtypeStruct((N_ALL, MV), BF16)),
        grid=(BATCH, steps),
        in_specs=specs(fwd) + specs(bwd),
        out_specs=(pl.BlockSpec((L, MV), lambda b, s: (fwd(b, s), 0)),
                   pl.BlockSpec((L, MV), lambda b, s: (bwd(b, s), 0))),
        scratch_shapes=[pltpu.VMEM((2 * MH, MDK, MDV + LANES), F32),
                        pltpu.VMEM((2 * MH, 8, LANES), F32)],
        compiler_params=_params(2),
        name="mlstm_scan",
    )(p, p, p, gc, gr, p, p, p, gc, gr)


def _mlstm_out_kernel(hf_ref, hb_ref, o_ref, xl_ref, xc_ref, gh_ref, gp_ref, gt_ref, w_ref,
                      out_ref, a_scr):
    is_latent = pl.program_id(0) < N_LAT // TM_MLSTM_OUT
    for r0 in range(0, TM_MLSTM_OUT, TM_OUT):
        rows = slice(r0, r0 + TM_OUT)
        for h in range(MH):
            sl = slice(h * MDV, (h + 1) * MDV)
            hh = hf_ref[rows, sl].astype(F32) + hb_ref[rows, sl].astype(F32)
            r = lax.rsqrt(jnp.mean(hh * hh, axis=-1, keepdims=True) + EPS)
            hn = (hh * r) * gh_ref[:, sl]
            a_scr[rows, sl] = (_sigmoid(o_ref[rows, sl].astype(F32)) * hn).astype(BF16)
        y = jnp.dot(a_scr[rows, :], w_ref[...], preferred_element_type=F32)
        x = jnp.where(is_latent, xl_ref[rows, :], xc_ref[rows, :])
        out_ref[rows, :] = _post_norm_residual(x, y, gp_ref[...], gt_ref[...])


def _mlstm_out(hf, hb, p, xl, xc, g_head, g_post, mod, w_out):
    tm = TM_MLSTM_OUT
    vec = pl.BlockSpec((1, D), lambda i: (0, 0))
    return pl.pallas_call(
        _mlstm_out_kernel,
        out_shape=jax.ShapeDtypeStruct((N_ALL, D), F32),
        grid=(N_ALL // tm,),
        in_specs=[
            pl.BlockSpec((tm, MV), lambda i: (i, 0)),
            pl.BlockSpec((tm, MV), lambda i: (i, 0)),
            pl.BlockSpec((tm, MV), lambda i: (i, 2)),
        ] + _two_source_specs(tm) + [
            vec, vec,
            pl.BlockSpec((None, 1, D), lambda i: (_mod_row(i, tm), 0, 2)),
            pl.BlockSpec((MV, D), lambda i: (0, 0), pipeline_mode=pl.Buffered(1)),
        ],
        out_specs=pl.BlockSpec((tm, D), lambda i: (i, 0)),
        scratch_shapes=[pltpu.VMEM((tm, MV), BF16)],
        compiler_params=_params(1),
        name="mlstm_out",
    )(hf, hb, p, xl, xc, g_head, g_post, mod, w_out)


def _ffn_kernel(x_ref, g_ref, sh_ref, sc_ref, gt_ref, gp_ref, wg_ref, wu_ref, wo_ref, out_ref,
                h_scr):
    f = pl.program_id(1)
    last_f = pl.num_programs(1) - 1
    half = TM_FFN // 2

    def step(first, last):
        for r in range(0, TM_FFN, half):
            rows = slice(r, r + half)
            if first:
                h_scr[rows, :] = _adaln(x_ref[rows, :], g_ref[...], sh_ref[...],
                                        sc_ref[...]).astype(BF16)
            h = h_scr[rows, :]
            gate = jnp.dot(h, wg_ref[...], preferred_element_type=F32)
            up = jnp.dot(h, wu_ref[...], preferred_element_type=F32)
            act = ((gate * _sigmoid(gate)) * up).astype(BF16)
            y = jnp.dot(act, wo_ref[...], preferred_element_type=F32)
            if not first:
                y = out_ref[rows, :] + y
            if last:
                y = _post_norm_residual(x_ref[rows, :], y, gp_ref[...], gt_ref[...])
            out_ref[rows, :] = y

    pl.when(f == 0)(functools.partial(step, True, False))
    pl.when((f > 0) & (f < last_f))(functools.partial(step, False, False))
    pl.when(f == last_f)(functools.partial(step, False, True))


def _ffn(x_rows, g_pre, g_post, mod, w_in, w_out, layer):
    n_rows = x_rows.shape[0]
    tm, tf = TM_FFN, TF_FFN
    nf = FFN // tf
    assert nf >= 2
    vec = pl.BlockSpec((1, D), lambda i, f: (0, 0))
    return pl.pallas_call(
        _ffn_kernel,
        out_shape=jax.ShapeDtypeStruct((n_rows, D), F32),
        grid=(n_rows // tm, nf),
        in_specs=[
            pl.BlockSpec((tm, D), lambda i, f: (i, 0)),
            vec, _mod_spec(tm, 3), _mod_spec(tm, 4), _mod_spec(tm, 5), vec,
            pl.BlockSpec((None, D, tf), lambda i, f: (layer, 0, f)),
            pl.BlockSpec((None, D, tf), lambda i, f: (layer, 0, nf + f)),
            pl.BlockSpec((None, tf, D), lambda i, f: (layer, f, 0)),
        ],
        out_specs=pl.BlockSpec((tm, D), lambda i, f: (i, 0)),
        scratch_shapes=[pltpu.VMEM((tm, D), BF16)],
        compiler_params=_params(2),
        name="ffn",
    )(x_rows, g_pre, mod, mod, mod, g_post, w_in, w_in, w_out)


def _norm_rope_heads(y_ref, out_ref, n_cols, a, b):
    for c in range(0, n_cols, ADH):
        yy = y_ref[:, c:c + ADH]
        r = lax.rsqrt(jnp.mean(yy * yy, axis=-1, keepdims=True) + EPS)
        out_ref[:, c:c + ADH] = ((yy * a + pltpu.roll(yy, ADH // 2, 1) * b) * r).astype(BF16)


def _qkv_kernel(x_ref, g_ref, sh_ref, sc_ref, aq_ref, bq_ref, ak_ref, bk_ref, w_ref,
                q_ref, k_ref, vt_ref, h_scr, y_scr):
    j = pl.program_id(1)
    n_q = AH * ADH // TN_QKV
    kv_cols = AKV * ADH

    def project(jj):
        y_scr[jj % 2] = jnp.dot(h_scr[...], w_ref[...], preferred_element_type=F32)

    def finish(jj):
        y_ref = y_scr.at[jj % 2]
        if jj < n_q:
            _norm_rope_heads(y_ref, q_ref, TN_QKV, aq_ref[...], bq_ref[...])
        else:
            _norm_rope_heads(y_ref, k_ref, kv_cols, ak_ref[...], bk_ref[...])
            vt = y_ref[:, kv_cols:].T
            ones = jnp.ones((VT_ONES, vt.shape[1]), BF16)
            for hd in range(AKV):
                vt_ref[hd * VT_ROWS:hd * VT_ROWS + ADH, :] = (
                    vt[hd * ADH:(hd + 1) * ADH, :].astype(BF16))
                vt_ref[hd * VT_ROWS + ADH:(hd + 1) * VT_ROWS, :] = ones

    def step(jj):
        if jj == 0:
            h_scr[...] = _adaln(x_ref[...], g_ref[...], sh_ref[...], sc_ref[...]).astype(BF16)
        if jj <= n_q:
            project(jj)
        if jj >= 1:
            finish(jj - 1)

    for jj in range(n_q + 2):
        pl.when(j == jj)(functools.partial(step, jj))


def _attn_qkv(x_all, gain, mod, a_q, b_q, a_k, b_k, w_qkv):
    tm, tn = TM_PROJ, TN_QKV
    n_q = AH * ADH // tn
    n_lat = N_LAT // tm
    per_seq = SEQ // tm
    kv_cols = AKV * ADH
    assert tn == 2 * kv_cols
    vec = pl.BlockSpec((1, D), lambda i, j: (0, 0))
    rope = pl.BlockSpec((tm, ADH), lambda i, j: (jnp.where(i < n_lat, i % per_seq, per_seq), 0))
    return pl.pallas_call(
        _qkv_kernel,
        out_shape=(jax.ShapeDtypeStruct((N_ALL, AH * ADH), BF16),
                   jax.ShapeDtypeStruct((N_ALL, kv_cols), BF16),
                   jax.ShapeDtypeStruct((AKV * VT_ROWS, N_ALL), BF16)),
        grid=(N_ALL // tm, n_q + 2),
        in_specs=[
            pl.BlockSpec((tm, D), lambda i, j: (i, 0)),
            vec, _mod_spec(tm, 0), _mod_spec(tm, 1), rope, rope, rope, rope,
            pl.BlockSpec((D, tn), lambda i, j: (0, jnp.minimum(j, n_q))),
        ],
        out_specs=(pl.BlockSpec((tm, tn), lambda i, j: (i, jnp.clip(j - 1, 0, n_q - 1))),
                   pl.BlockSpec((tm, kv_cols), lambda i, j: (i, 0)),
                   pl.BlockSpec((AKV * VT_ROWS, tm), lambda i, j: (0, i))),
        scratch_shapes=[pltpu.VMEM((tm, D), BF16), pltpu.VMEM((2, tm, tn), F32)],
        compiler_params=_params(2),
        name="attn_qkv",
    )(x_all, gain, mod, mod, a_q, b_q, a_k, b_k, w_qkv)


def _attn_kernel(q_ref, kl_ref, kc_ref, vtl_ref, vtc_ref, o_ref):
    tq = q_ref.shape[0]
    nq = AG * tq
    q = jnp.concatenate([q_ref[:, g * ADH:(g + 1) * ADH] for g in range(AG)], axis=0)

    def block(k, vt, m, acc):
        st = lax.dot_general(k, q, _NT, preferred_element_type=F32)
        m_new = jnp.maximum(m, jnp.max(st, axis=0, keepdims=True))
        p = jnp.exp2(st - m_new).astype(BF16)
        acc = jnp.exp2(m - m_new) * acc + jnp.dot(vt, p, preferred_element_type=F32)
        return m_new, acc

    m = jnp.full((1, nq), -jnp.inf, F32)
    acc = jnp.zeros((VT_ROWS, nq), F32)
    for c in range(SEQ // TK_ATT):
        m, acc = block(kl_ref[c * TK_ATT:(c + 1) * TK_ATT, :],
                       vtl_ref[:, c * TK_ATT:(c + 1) * TK_ATT], m, acc)
    m, acc = block(kc_ref[...], vtc_ref[...], m, acc)
    out = (acc[:ADH, :] * (1.0 / acc[ADH:ADH + 1, :])).T
    for g in range(AG):
        o_ref[:, g * ADH:(g + 1) * ADH] = out[g * tq:(g + 1) * tq, :].astype(BF16)


def _attention(q, k, vt):
    tq = TQ_ATT
    per_seq = SEQ // tq
    return pl.pallas_call(
        _attn_kernel,
        out_shape=jax.ShapeDtypeStruct((N_LAT, AH * ADH), BF16),
        grid=(BATCH, AKV, per_seq),
        in_specs=[
            pl.BlockSpec((tq, AG * ADH), lambda b, h, i: (b * per_seq + i, h)),
            pl.BlockSpec((SEQ, ADH), lambda b, h, i: (b, h)),
            pl.BlockSpec((CTX, ADH), lambda b, h, i: (N_LAT // CTX + b, h)),
            pl.BlockSpec((VT_ROWS, SEQ), lambda b, h, i: (h, b)),
            pl.BlockSpec((VT_ROWS, CTX), lambda b, h, i: (h, N_LAT // CTX + b)),
        ],
        out_specs=pl.BlockSpec((tq, AG * ADH), lambda b, h, i: (b * per_seq + i, h)),
        compiler_params=_params(3),
        name="attention",
    )(q, k, k, vt, vt)


def _attn_out_kernel(o_ref, x_ref, gp_ref, gt_ref, w_ref, out_ref):
    for r in range(0, TM_ATT_OUT, TM_OUT):
        rows = slice(r, r + TM_OUT)
        y = jnp.dot(o_ref[rows, :], w_ref[...], preferred_element_type=F32)
        out_ref[rows, :] = _post_norm_residual(x_ref[rows, :], y, gp_ref[...], gt_ref[...])


def _attn_out(o, x_all, g_post, mod, w_out):
    tm = TM_ATT_OUT
    return pl.pallas_call(
        _attn_out_kernel,
        out_shape=jax.ShapeDtypeStruct((N_LAT, D), F32),
        grid=(N_LAT // tm,),
        in_specs=[
            pl.BlockSpec((tm, D), lambda i: (i, 0)),
            pl.BlockSpec((tm, D), lambda i: (i, 0)),
            pl.BlockSpec((1, D), lambda i: (0, 0)),
            pl.BlockSpec((None, 1, D), lambda i: (_mod_row(i, tm), 0, 2)),
            pl.BlockSpec((D, D), lambda i: (0, 0)),
        ],
        out_specs=pl.BlockSpec((tm, D), lambda i: (i, 0)),
        compiler_params=_params(1),
        name="attn_out",
    )(o, x_all, g_post, mod, w_out)


def _half_split(a):
    lead = a.shape[:-1]
    n = a.shape[-1] // ADH
    a = a.reshape(*lead, n, ADH // 2, 2)
    return jnp.swapaxes(a, -1, -2).reshape(*lead, n * ADH)


def _qkv_weight_kernel(w_ref, o_ref):
    first_head = pl.program_id(0) * (TN_WPERM // ADH)
    src = lax.broadcasted_iota(jnp.int32, (ADH, ADH), 0)
    dst = lax.broadcasted_iota(jnp.int32, (ADH, ADH), 1)
    half = ADH // 2
    split_src = jnp.where(dst < half, 2 * dst, 2 * (dst - half) + 1)
    for h in range(TN_WPERM // ADH):
        is_qk = first_head + h < AH + AKV
        perm = (src == jnp.where(is_qk, split_src, dst)).astype(BF16)
        cols = slice(h * ADH, (h + 1) * ADH)
        o_ref[:, cols] = jnp.dot(w_ref[:, cols].astype(BF16), perm,
                                 preferred_element_type=F32).astype(BF16)


def _qkv_weight(w_qkv):
    n_cols = w_qkv.shape[1]
    return pl.pallas_call(
        _qkv_weight_kernel,
        out_shape=jax.ShapeDtypeStruct((D, n_cols), BF16),
        grid=(n_cols // TN_WPERM,),
        in_specs=[pl.BlockSpec((D, TN_WPERM), lambda j: (0, j))],
        out_specs=pl.BlockSpec((D, TN_WPERM), lambda j: (0, j)),
        compiler_params=_params(1),
        name="qkv_weight",
    )(w_qkv)


def _rope_tables(gain, scale):
    rows = SEQ // GRID_W
    t_row = jnp.repeat(jnp.arange(rows, dtype=F32), GRID_W)
    t_col = jnp.tile(jnp.arange(GRID_W, dtype=F32), rows)
    per_axis = ADH // 2
    inv = ROPE_THETA ** (-jnp.arange(0, per_axis, 2, dtype=F32) / per_axis)
    ang = jnp.concatenate([t_row[:, None] * inv, t_col[:, None] * inv], axis=-1)
    ang = jnp.concatenate([ang, jnp.zeros((TM_PROJ, per_axis), F32)], axis=0)
    cos = jnp.concatenate([jnp.cos(ang), jnp.cos(ang)], axis=-1)
    sin = jnp.concatenate([-jnp.sin(ang), jnp.sin(ang)], axis=-1)
    g = _half_split(gain.astype(F32))
    return (g * scale) * cos, (jnp.roll(g, per_axis) * scale) * sin


def kernel(x, c, ctx, c_ctx, w_mod, b_mod, g_mix_pre, g_mix_post, g_ffn_pre, g_ffn_post,
           w_mlstm_in, b_mlstm_gate, g_mlstm_head, w_mlstm_out,
           w_attn_qkv, g_attn_q, g_attn_k, w_attn_out, w_ffn_in, w_ffn_out):
    assert x.shape == (BATCH, SEQ, D) and ctx.shape == (BATCH, CTX, D)
    xl = x.reshape(N_LAT, D)
    xc = ctx.reshape(N_CTX, D)
    row = lambda a: a.reshape(1, -1)

    c_all = jnp.concatenate([c, c_ctx[None, :], jnp.zeros((MOD_ROWS - BATCH - 1, D), F32)], axis=0)
    mod = _modulation(c_all, w_mod, b_mod).reshape(2, MOD_ROWS, 1, N_MOD * D)

    w_main = w_mlstm_in[0].astype(BF16)
    w_gate = jnp.pad(w_main[:, M_MAIN:], ((0, 0), (0, GATE_PAD - M_GATES)))
    b_gate = jnp.pad(b_mlstm_gate[0], (0, GATE_PAD - M_GATES)).reshape(1, GATE_PAD)
    p, gpre = _mlstm_inproj(xl, xc, row(g_mix_pre[0]), mod[0], w_main, w_gate)
    gc, gr = _gateprep(gpre, b_gate)
    hf, hb = _mlstm_scan(p, gc, gr)
    x_all = _mlstm_out(hf, hb, p, xl, xc, row(g_mlstm_head[0]), row(g_mix_post[0]), mod[0],
                       w_mlstm_out[0].astype(BF16))
    w_ffn_in16 = w_ffn_in.astype(BF16)
    w_ffn_out16 = w_ffn_out.astype(BF16)
    x_all = _ffn(x_all, row(g_ffn_pre[0]), row(g_ffn_post[0]), mod[0], w_ffn_in16, w_ffn_out16, 0)

    w_qkv = _qkv_weight(w_attn_qkv[0])
    a_q, b_q = _rope_tables(g_attn_q[0], A_SCALE * LOG2E)
    a_k, b_k = _rope_tables(g_attn_k[0], 1.0)
    q, k, vt = _attn_qkv(x_all, row(g_mix_pre[1]), mod[1], a_q, b_q, a_k, b_k, w_qkv)
    o = _attention(q, k, vt)
    x_lat = _attn_out(o, x_all, row(g_mix_post[1]), mod[1], w_attn_out[0].astype(BF16))
    x_lat = _ffn(x_lat, row(g_ffn_pre[1]), row(g_ffn_post[1]), mod[1], w_ffn_in16, w_ffn_out16, 1)
    return x_lat.reshape(BATCH, SEQ, D)
```

```python
import functools
import math

import jax
import jax.numpy as jnp
from jax import lax
from jax.experimental import pallas as pl
from jax.experimental.pallas import tpu as pltpu

F32 = jnp.float32
BF16 = jnp.bfloat16

D = 2048
BATCH = 4
SEQ = 4096
CTX = 256
N_LAT = BATCH * SEQ
N_CTX = BATCH * CTX
N_ALL = N_LAT + N_CTX
N_MOD = 6
EPS = 1e-6
MOD_ROWS = 8

MH = 8
MDK = 128
MDV = 256
MQK = MH * MDK
MV = MH * MDV
M_MAIN = 2 * MQK + 2 * MV
M_GATES = 4 * MH
GATE_PAD = 128
SOFTCAP = 15.0
M_SCALE_LOG2 = -0.5 * math.log2(MDK)
LANES = 128

AH = 16
AKV = 4
ADH = 128
AG = AH // AKV
A_SCALE = ADH ** -0.5
LOG2E = math.log2(math.e)
VT_ONES = 16
VT_ROWS = ADH + VT_ONES
GRID_W = 64
ROPE_THETA = 10000.0

FFN = 5632

VMEM_LIMIT = 56 * 1024 * 1024
TM_PROJ = 512
TM_QKV = 1024
TN_INPROJ = 2048
TN_QKV = 1024
TM_OUT = 256
TM_ATT_OUT = 2 * TM_OUT
TM_MLSTM_OUT = 2 * TM_OUT
TM_FFN = 1024
TF_FFN = 512
TN_MOD = 1024
TN_WPERM = 512
L_SCAN = 256
GATEPREP_CHUNKS = 4
TQ_ATT = 512
TK_ATT = 1024

_NT = (((1,), (1,)), ((), ()))
_TN = (((0,), (0,)), ((), ()))


def _params(n_axes):
    return pltpu.CompilerParams(
        dimension_semantics=("arbitrary",) * n_axes, vmem_limit_bytes=VMEM_LIMIT)


def _mod_row(i, tm):
    return jnp.where(i < N_LAT // tm, i // (SEQ // tm), BATCH)


def _mod_spec(tm, chunk):
    return pl.BlockSpec((None, 1, D), lambda i, j: (_mod_row(i, tm), 0, chunk))


def _adaln(x, gain, shift, scale):
    r = lax.rsqrt(jnp.mean(x * x, axis=-1, keepdims=True) + EPS)
    return (x * r) * (gain * (1.0 + scale)) + shift


def _post_norm_residual(x, y, gain, gate):
    r = lax.rsqrt(jnp.mean(y * y, axis=-1, keepdims=True) + EPS)
    return x + (y * r) * (gain * gate)


def _sigmoid(z):
    return 1.0 / (1.0 + jnp.exp2(z * (-LOG2E)))


def _mod_kernel(c_ref, w_ref, b_ref, o_ref):
    c = c_ref[...]
    cond = (c * _sigmoid(c)).astype(BF16)
    o_ref[...] = jnp.dot(cond, w_ref[...].astype(BF16), preferred_element_type=F32) + b_ref[...]


def _modulation(c_all, w_mod, b_mod):
    depth = w_mod.shape[0]
    return pl.pallas_call(
        _mod_kernel,
        out_shape=jax.ShapeDtypeStruct((depth, MOD_ROWS, N_MOD * D), F32),
        grid=(depth, N_MOD * D // TN_MOD),
        in_specs=[
            pl.BlockSpec((MOD_ROWS, D), lambda l, j: (0, 0)),
            pl.BlockSpec((None, D, TN_MOD), lambda l, j: (l, 0, j)),
            pl.BlockSpec((None, 1, TN_MOD), lambda l, j: (l, 0, j)),
        ],
        out_specs=pl.BlockSpec((None, MOD_ROWS, TN_MOD), lambda l, j: (l, 0, j)),
        compiler_params=_params(2),
        name="modulation",
    )(c_all, w_mod, b_mod.reshape(depth, 1, N_MOD * D))


def _inproj_kernel(xl_ref, xc_ref, g_ref, sh_ref, sc_ref, w_ref, wg_ref, p_ref, gpre_ref, h_scr):
    i = pl.program_id(0)
    j = pl.program_id(1)

    @pl.when(j == 0)
    def _():
        is_latent = i < N_LAT // TM_PROJ
        half = TM_PROJ // 2
        for r in range(0, TM_PROJ, half):
            rows = slice(r, r + half)
            x = jnp.where(is_latent, xl_ref[rows, :], xc_ref[rows, :])
            h = _adaln(x, g_ref[...], sh_ref[...], sc_ref[...]).astype(BF16)
            h_scr[rows, :] = h
            gpre_ref[rows, :] = jnp.dot(h, wg_ref[...], preferred_element_type=F32)
            p_ref[rows, :] = jnp.dot(h, w_ref[...], preferred_element_type=F32).astype(BF16)

    @pl.when(j > 0)
    def _():
        p_ref[...] = jnp.dot(h_scr[...], w_ref[...], preferred_element_type=F32).astype(BF16)


def _two_source_specs(tm):
    n_lat = N_LAT // tm
    return [
        pl.BlockSpec((tm, D), lambda i, *_: (jnp.minimum(i, n_lat - 1), 0)),
        pl.BlockSpec((tm, D), lambda i, *_: (jnp.maximum(i - n_lat, 0), 0)),
    ]


def _mlstm_inproj(xl, xc, gain, mod, w_main, w_gate):
    tm, tn = TM_PROJ, TN_INPROJ
    vec = pl.BlockSpec((1, D), lambda i, j: (0, 0))
    return pl.pallas_call(
        _inproj_kernel,
        out_shape=(jax.ShapeDtypeStruct((N_ALL, M_MAIN), BF16),
                   jax.ShapeDtypeStruct((N_ALL, GATE_PAD), F32)),
        grid=(N_ALL // tm, M_MAIN // tn),
        in_specs=_two_source_specs(tm) + [
            vec, _mod_spec(tm, 0), _mod_spec(tm, 1),
            pl.BlockSpec((D, tn), lambda i, j: (0, j)),
            pl.BlockSpec((D, GATE_PAD), lambda i, j: (0, 0)),
        ],
        out_specs=(pl.BlockSpec((tm, tn), lambda i, j: (i, j)),
                   pl.BlockSpec((tm, GATE_PAD), lambda i, j: (i, 0))),
        scratch_shapes=[pltpu.VMEM((tm, D), BF16)],
        compiler_params=_params(2),
        name="mlstm_inproj",
    )(xl, xc, gain, mod, mod, w_main, w_gate)


def _gateprep_kernel(gpre_ref, b_ref, gc_ref, gr_ref):
    for c in range(GATEPREP_CHUNKS):
        rows = slice(c * L_SCAN, (c + 1) * L_SCAN)
        _gateprep_chunk(gpre_ref.at[rows, :], b_ref, gc_ref.at[rows, :], gr_ref.at[:, rows])


def _gateprep_chunk(gpre_ref, b_ref, gc_ref, gr_ref):
    L = gpre_ref.shape[0]
    z = gpre_ref[...] + b_ref[...]
    a = SOFTCAP * jnp.tanh(z * (1.0 / SOFTCAP))
    logsig = jnp.minimum(a, 0.0) - jnp.log(1.0 + jnp.exp(-jnp.abs(a)))
    row = lax.broadcasted_iota(jnp.int32, (L, GATE_PAD), 0)
    lane = lax.broadcasted_iota(jnp.int32, (L, GATE_PAD), 1)

    def scans(x, op, fill):
        pre, suf = x, x
        s = 1
        while s < L:
            pre = op(pre, jnp.where(row >= s, pltpu.roll(pre, s, 0), fill))
            suf = op(suf, jnp.where(row < L - s, pltpu.roll(suf, L - s, 0), fill))
            s *= 2
        return pre, suf

    fwd_lanes = lane < 2 * MH
    b_pre, b_suf = scans(logsig, jnp.add, 0.0)
    b = jnp.where(fwd_lanes, b_pre, b_suf)
    r = a - pltpu.roll(b, GATE_PAD - MH, 1)
    c_pre, c_suf = scans(r, jnp.maximum, -jnp.inf)
    cmax = jnp.where(fwd_lanes, c_pre, c_suf)
    is_gate_lane = (lane < MH) | ((lane >= 2 * MH) & (lane < 3 * MH))
    gc_ref[...] = jnp.where(is_gate_lane, cmax, b) * LOG2E
    gr_ref[...] = (r * LOG2E).T[:M_GATES, :]


def _gateprep(gpre, bias):
    L = GATEPREP_CHUNKS * L_SCAN
    return pl.pallas_call(
        _gateprep_kernel,
        out_shape=(jax.ShapeDtypeStruct((N_ALL, GATE_PAD), F32),
                   jax.ShapeDtypeStruct((M_GATES, N_ALL), F32)),
        grid=(N_ALL // L,),
        in_specs=[pl.BlockSpec((L, GATE_PAD), lambda i: (i, 0)),
                  pl.BlockSpec((1, GATE_PAD), lambda i: (0, 0))],
        out_specs=(pl.BlockSpec((L, GATE_PAD), lambda i: (i, 0)),
                   pl.BlockSpec((M_GATES, L), lambda i: (0, i))),
        compiler_params=_params(1),
        name="mlstm_gateprep",
    )(gpre, bias)


def _scan_unit(q, k, vx, r_row, cmax_col, b_col, mask, end, cx_ref, m_ref, h_out):
    L = q.shape[0]
    m = m_ref[0:1, 0:1]
    u = jnp.maximum(m, cmax_col)
    u_b = jnp.broadcast_to(u - M_SCALE_LOG2, (L, L))
    dmat = jnp.where(mask, jnp.exp2(r_row - u_b), 0.0)
    sm = (lax.dot_general(q, k, _NT, preferred_element_type=F32) * dmat).astype(BF16)
    u_end = u[end:end + 1, :]
    w_row = jnp.exp2(r_row - u_end)
    kw = (k.astype(F32).T * w_row).astype(BF16)
    both = jnp.dot(jnp.concatenate([sm, kw], axis=0), vx, preferred_element_type=F32)
    intra = both[:L, :]
    cx = cx_ref[...]
    inter = jnp.dot(q, cx.astype(BF16), preferred_element_type=F32)
    u_rep = u_b[:, :LANES]
    w_inter = jnp.exp2(m - u_rep)
    den = intra[:, MDV:] + w_inter * inter[:, MDV:]
    floor = jnp.exp2(-(jnp.broadcast_to(b_col, (L, LANES)) + u_rep + M_SCALE_LOG2))
    inv = 1.0 / jnp.maximum(jnp.abs(den), floor)
    for t in range(MDV // LANES):
        cols = slice(t * LANES, (t + 1) * LANES)
        h_out[:, cols] = ((intra[:, cols] + w_inter * inter[:, cols]) * inv).astype(h_out.dtype)

    cx_ref[...] = jnp.exp2(m - u_end) * cx + both[L:, :]
    m_ref[...] = jnp.broadcast_to(b_col[end:end + 1, :] + u_end, m_ref.shape)


def _scan_kernel(qf, kf, vf, gcf, grf, qb, kb, vb, gcb, grb, hf_ref, hb_ref, cx_scr, m_scr):
    L = qf.shape[0]

    @pl.when(pl.program_id(1) == 0)
    def _():
        cx_scr[...] = jnp.zeros_like(cx_scr)
        m_scr[...] = jnp.zeros_like(m_scr)

    row = lax.broadcasted_iota(jnp.int32, (L, L), 0)
    col = lax.broadcasted_iota(jnp.int32, (L, L), 1)
    ones = jnp.ones((L, LANES), BF16)
    directions = (
        (qf, kf, vf, gcf, grf, hf_ref, col <= row, 0, L - 1),
        (qb, kb, vb, gcb, grb, hb_ref, col >= row, 2 * MH, 0),
    )
    for d, (q_ref, k_ref, v_ref, gc_ref, gr_ref, h_ref, mask, goff, end) in enumerate(directions):
        for h in range(MH):
            ig, fg = goff + h, goff + MH + h
            _scan_unit(
                q_ref[:, h * MDK:(h + 1) * MDK], k_ref[:, h * MDK:(h + 1) * MDK],
                jnp.concatenate([v_ref[:, h * MDV:(h + 1) * MDV], ones], axis=1),
                gr_ref[ig:ig + 1, :], gc_ref[:, ig:ig + 1], gc_ref[:, fg:fg + 1],
                mask, end, cx_scr.at[d * MH + h], m_scr.at[d * MH + h],
                h_ref.at[:, h * MDV:(h + 1) * MDV])


def _mlstm_scan(p, gc, gr):
    L = L_SCAN
    n_ctx_chunks = CTX // L
    n_lat_chunks = SEQ // L
    steps = n_ctx_chunks + n_lat_chunks
    ctx0 = N_LAT // L

    def fwd(b, s):
        return jnp.where(s < n_ctx_chunks, ctx0 + b * n_ctx_chunks + s,
                         b * n_lat_chunks + s - n_ctx_chunks)

    def bwd(b, s):
        return jnp.where(s < n_ctx_chunks, ctx0 + b * n_ctx_chunks + (n_ctx_chunks - 1 - s),
                         b * n_lat_chunks + (n_lat_chunks - 1 - (s - n_ctx_chunks)))

    def specs(idx):
        return [
            pl.BlockSpec((L, MQK), lambda b, s: (idx(b, s), 0)),
            pl.BlockSpec((L, MQK), lambda b, s: (idx(b, s), 1)),
            pl.BlockSpec((L, MV), lambda b, s: (idx(b, s), 1)),
            pl.BlockSpec((L, GATE_PAD), lambda b, s: (idx(b, s), 0)),
            pl.BlockSpec((M_GATES, L), lambda b, s: (0, idx(b, s))),
        ]

    return pl.pallas_call(
        _scan_kernel,
        out_shape=(jax.ShapeDtypeStruct((N_ALL, MV), BF16),
                   jax.ShapeDtypeStruct((N_ALL, MV), BF16)),
        grid=(BATCH, steps),
        in_specs=specs(fwd) + specs(bwd),
        out_specs=(pl.BlockSpec((L, MV), lambda b, s: (fwd(b, s), 0)),
                   pl.BlockSpec((L, MV), lambda b, s: (bwd(b, s), 0))),
        scratch_shapes=[pltpu.VMEM((2 * MH, MDK, MDV + LANES), F32),
                        pltpu.VMEM((2 * MH, 8, LANES), F32)],
        compiler_params=_params(2),
        name="mlstm_scan",
    )(p, p, p, gc, gr, p, p, p, gc, gr)


def _mlstm_out_kernel(hf_ref, hb_ref, o_ref, xl_ref, xc_ref, gh_ref, gp_ref, gt_ref, w_ref,
                      out_ref, a_scr):
    is_latent = pl.program_id(0) < N_LAT // TM_MLSTM_OUT
    for r0 in range(0, TM_MLSTM_OUT, TM_OUT):
        rows = slice(r0, r0 + TM_OUT)
        for h in range(MH):
            sl = slice(h * MDV, (h + 1) * MDV)
            hh = hf_ref[rows, sl].astype(F32) + hb_ref[rows, sl].astype(F32)
            r = lax.rsqrt(jnp.mean(hh * hh, axis=-1, keepdims=True) + EPS)
            hn = (hh * r) * gh_ref[:, sl]
            a_scr[rows, sl] = (_sigmoid(o_ref[rows, sl].astype(F32)) * hn).astype(BF16)
        y = jnp.dot(a_scr[rows, :], w_ref[...], preferred_element_type=F32)
        x = jnp.where(is_latent, xl_ref[rows, :], xc_ref[rows, :])
        out_ref[rows, :] = _post_norm_residual(x, y, gp_ref[...], gt_ref[...])


def _mlstm_out(hf, hb, p, xl, xc, g_head, g_post, mod, w_out):
    tm = TM_MLSTM_OUT
    vec = pl.BlockSpec((1, D), lambda i: (0, 0))
    return pl.pallas_call(
        _mlstm_out_kernel,
        out_shape=jax.ShapeDtypeStruct((N_ALL, D), F32),
        grid=(N_ALL // tm,),
        in_specs=[
            pl.BlockSpec((tm, MV), lambda i: (i, 0)),
            pl.BlockSpec((tm, MV), lambda i: (i, 0)),
            pl.BlockSpec((tm, MV), lambda i: (i, 2)),
        ] + _two_source_specs(tm) + [
            vec, vec,
            pl.BlockSpec((None, 1, D), lambda i: (_mod_row(i, tm), 0, 2)),
            pl.BlockSpec((MV, D), lambda i: (0, 0), pipeline_mode=pl.Buffered(1)),
        ],
        out_specs=pl.BlockSpec((tm, D), lambda i: (i, 0)),
        scratch_shapes=[pltpu.VMEM((tm, MV), BF16)],
        compiler_params=_params(1),
        name="mlstm_out",
    )(hf, hb, p, xl, xc, g_head, g_post, mod, w_out)


def _ffn_kernel(x_ref, g_ref, sh_ref, sc_ref, gt_ref, gp_ref, wg_ref, wu_ref, wo_ref, out_ref,
                h_scr):
    f = pl.program_id(1)
    last_f = pl.num_programs(1) - 1
    half = TM_FFN // 2

    def step(first, last):
        for r in range(0, TM_FFN, half):
            rows = slice(r, r + half)
            if first:
                h_scr[rows, :] = _adaln(x_ref[rows, :], g_ref[...], sh_ref[...],
                                        sc_ref[...]).astype(BF16)
            h = h_scr[rows, :]
            gate = jnp.dot(h, wg_ref[...], preferred_element_type=F32)
            up = jnp.dot(h, wu_ref[...], preferred_element_type=F32)
            act = ((gate * _sigmoid(gate)) * up).astype(BF16)
            y = jnp.dot(act, wo_ref[...], preferred_element_type=F32)
            if not first:
                y = out_ref[rows, :] + y
            if last:
                y = _post_norm_residual(x_ref[rows, :], y, gp_ref[...], gt_ref[...])
            out_ref[rows, :] = y

    pl.when(f == 0)(functools.partial(step, True, False))
    pl.when((f > 0) & (f < last_f))(functools.partial(step, False, False))
    pl.when(f == last_f)(functools.partial(step, False, True))


def _ffn(x_rows, g_pre, g_post, mod, w_in, w_out, layer):
    n_rows = x_rows.shape[0]
    tm, tf = TM_FFN, TF_FFN
    nf = FFN // tf
    assert nf >= 2
    vec = pl.BlockSpec((1, D), lambda i, f: (0, 0))
    return pl.pallas_call(
        _ffn_kernel,
        out_shape=jax.ShapeDtypeStruct((n_rows, D), F32),
        grid=(n_rows // tm, nf),
        in_specs=[
            pl.BlockSpec((tm, D), lambda i, f: (i, 0)),
            vec, _mod_spec(tm, 3), _mod_spec(tm, 4), _mod_spec(tm, 5), vec,
            pl.BlockSpec((None, D, tf), lambda i, f: (layer, 0, f)),
            pl.BlockSpec((None, D, tf), lambda i, f: (layer, 0, nf + f)),
            pl.BlockSpec((None, tf, D), lambda i, f: (layer, f, 0)),
        ],
        out_specs=pl.BlockSpec((tm, D), lambda i, f: (i, 0)),
        scratch_shapes=[pltpu.VMEM((tm, D), BF16)],
        compiler_params=_params(2),
        name="ffn",
    )(x_rows, g_pre, mod, mod, mod, g_post, w_in, w_in, w_out)


def _norm_rope_heads(y_ref, out_ref, n_cols, a, b):
    for c in range(0, n_cols, ADH):
        yy = y_ref[:, c:c + ADH]
        r = lax.rsqrt(jnp.mean(yy * yy, axis=-1, keepdims=True) + EPS)
        out_ref[:, c:c + ADH] = ((yy * a + pltpu.roll(yy, ADH // 2, 1) * b) * r).astype(BF16)


def _qkv_kernel(x_ref, g_ref, sh_ref, sc_ref, aq_ref, bq_ref, ak_ref, bk_ref, w_ref,
                q_ref, k_ref, vt_ref, h_scr, y_scr):
    j = pl.program_id(1)
    n_q = AH * ADH // TN_QKV
    kv_cols = AKV * ADH

    def project(jj):
        y_scr[jj % 2] = jnp.dot(h_scr[...], w_ref[...], preferred_element_type=F32)

    def finish(jj):
        y_ref = y_scr.at[jj % 2]
        if jj < n_q:
            _norm_rope_heads(y_ref, q_ref, TN_QKV, aq_ref[...], bq_ref[...])
        else:
            _norm_rope_heads(y_ref, k_ref, kv_cols, ak_ref[...], bk_ref[...])
            vt = y_ref[:, kv_cols:].T
            ones = jnp.ones((VT_ONES, vt.shape[1]), BF16)
            for hd in range(AKV):
                vt_ref[hd * VT_ROWS:hd * VT_ROWS + ADH, :] = (
                    vt[hd * ADH:(hd + 1) * ADH, :].astype(BF16))
                vt_ref[hd * VT_ROWS + ADH:(hd + 1) * VT_ROWS, :] = ones

    def step(jj):
        if jj == 0:
            h_scr[...] = _adaln(x_ref[...], g_ref[...], sh_ref[...], sc_ref[...]).astype(BF16)
        if jj <= n_q:
            project(jj)
        if jj >= 1:
            finish(jj - 1)

    for jj in range(n_q + 2):
        pl.when(j == jj)(functools.partial(step, jj))


def _attn_qkv(x_all, gain, mod, a_q, b_q, a_k, b_k, w_qkv):
    tm, tn = TM_QKV, TN_QKV
    n_q = AH * ADH // tn
    n_lat = N_LAT // tm
    per_seq = SEQ // tm
    kv_cols = AKV * ADH
    assert tn == 2 * kv_cols
    vec = pl.BlockSpec((1, D), lambda i, j: (0, 0))
    rope = pl.BlockSpec((tm, ADH), lambda i, j: (jnp.where(i < n_lat, i % per_seq, per_seq), 0))
    return pl.pallas_call(
        _qkv_kernel,
        out_shape=(jax.ShapeDtypeStruct((N_ALL, AH * ADH), BF16),
                   jax.ShapeDtypeStruct((N_ALL, kv_cols), BF16),
                   jax.ShapeDtypeStruct((AKV * VT_ROWS, N_ALL), BF16)),
        grid=(N_ALL // tm, n_q + 2),
        in_specs=[
            pl.BlockSpec((tm, D), lambda i, j: (i, 0)),
            vec, _mod_spec(tm, 0), _mod_spec(tm, 1), rope, rope, rope, rope,
            pl.BlockSpec((D, tn), lambda i, j: (0, jnp.minimum(j, n_q))),
        ],
        out_specs=(pl.BlockSpec((tm, tn), lambda i, j: (i, jnp.clip(j - 1, 0, n_q - 1))),
                   pl.BlockSpec((tm, kv_cols), lambda i, j: (i, 0)),
                   pl.BlockSpec((AKV * VT_ROWS, tm), lambda i, j: (0, i))),
        scratch_shapes=[pltpu.VMEM((tm, D), BF16), pltpu.VMEM((2, tm, tn), F32)],
        compiler_params=_params(2),
        name="attn_qkv",
    )(x_all, gain, mod, mod, a_q, b_q, a_k, b_k, w_qkv)


def _attn_kernel(q_ref, kl_ref, kc_ref, vtl_ref, vtc_ref, o_ref):
    tq = q_ref.shape[0]
    nq = AG * tq
    q = jnp.concatenate([q_ref[:, g * ADH:(g + 1) * ADH] for g in range(AG)], axis=0)

    def block(k, vt, m, acc):
        st = lax.dot_general(k, q, _NT, preferred_element_type=F32)
        m_new = jnp.maximum(m, jnp.max(st, axis=0, keepdims=True))
        p = jnp.exp2(st - m_new).astype(BF16)
        acc = jnp.exp2(m - m_new) * acc + jnp.dot(vt, p, preferred_element_type=F32)
        return m_new, acc

    m = jnp.full((1, nq), -jnp.inf, F32)
    acc = jnp.zeros((VT_ROWS, nq), F32)
    for c in range(SEQ // TK_ATT):
        m, acc = block(kl_ref[c * TK_ATT:(c + 1) * TK_ATT, :],
                       vtl_ref[:, c * TK_ATT:(c + 1) * TK_ATT], m, acc)
    m, acc = block(kc_ref[...], vtc_ref[...], m, acc)
    out = (acc[:ADH, :] * (1.0 / acc[ADH:ADH + 1, :])).T
    for g in range(AG):
        o_ref[:, g * ADH:(g + 1) * ADH] = out[g * tq:(g + 1) * tq, :].astype(BF16)


def _attention(q, k, vt):
    tq = TQ_ATT
    per_seq = SEQ // tq
    return pl.pallas_call(
        _attn_kernel,
        out_shape=jax.ShapeDtypeStruct((N_LAT, AH * ADH), BF16),
        grid=(BATCH, AKV, per_seq),
        in_specs=[
            pl.BlockSpec((tq, AG * ADH), lambda b, h, i: (b * per_seq + i, h)),
            pl.BlockSpec((SEQ, ADH), lambda b, h, i: (b, h)),
            pl.BlockSpec((CTX, ADH), lambda b, h, i: (N_LAT // CTX + b, h)),
            pl.BlockSpec((VT_ROWS, SEQ), lambda b, h, i: (h, b)),
            pl.BlockSpec((VT_ROWS, CTX), lambda b, h, i: (h, N_LAT // CTX + b)),
        ],
        out_specs=pl.BlockSpec((tq, AG * ADH), lambda b, h, i: (b * per_seq + i, h)),
        compiler_params=_params(3),
        name="attention",
    )(q, k, k, vt, vt)


def _attn_out_kernel(o_ref, x_ref, gp_ref, gt_ref, w_ref, out_ref):
    for r in range(0, TM_ATT_OUT, TM_OUT):
        rows = slice(r, r + TM_OUT)
        y = jnp.dot(o_ref[rows, :], w_ref[...], preferred_element_type=F32)
        out_ref[rows, :] = _post_norm_residual(x_ref[rows, :], y, gp_ref[...], gt_ref[...])


def _attn_out(o, x_all, g_post, mod, w_out):
    tm = TM_ATT_OUT
    return pl.pallas_call(
        _attn_out_kernel,
        out_shape=jax.ShapeDtypeStruct((N_LAT, D), F32),
        grid=(N_LAT // tm,),
        in_specs=[
            pl.BlockSpec((tm, D), lambda i: (i, 0)),
            pl.BlockSpec((tm, D), lambda i: (i, 0)),
            pl.BlockSpec((1, D), lambda i: (0, 0)),
            pl.BlockSpec((None, 1, D), lambda i: (_mod_row(i, tm), 0, 2)),
            pl.BlockSpec((D, D), lambda i: (0, 0)),
        ],
        out_specs=pl.BlockSpec((tm, D), lambda i: (i, 0)),
        compiler_params=_params(1),
        name="attn_out",
    )(o, x_all, g_post, mod, w_out)


def _half_split(a):
    lead = a.shape[:-1]
    n = a.shape[-1] // ADH
    a = a.reshape(*lead, n, ADH // 2, 2)
    return jnp.swapaxes(a, -1, -2).reshape(*lead, n * ADH)


def _qkv_weight_kernel(w_ref, o_ref):
    first_head = pl.program_id(0) * (TN_WPERM // ADH)
    src = lax.broadcasted_iota(jnp.int32, (ADH, ADH), 0)
    dst = lax.broadcasted_iota(jnp.int32, (ADH, ADH), 1)
    half = ADH // 2
    split_src = jnp.where(dst < half, 2 * dst, 2 * (dst - half) + 1)
    for h in range(TN_WPERM // ADH):
        is_qk = first_head + h < AH + AKV
        perm = (src == jnp.where(is_qk, split_src, dst)).astype(BF16)
        cols = slice(h * ADH, (h + 1) * ADH)
        o_ref[:, cols] = jnp.dot(w_ref[:, cols].astype(BF16), perm,
                                 preferred_element_type=F32).astype(BF16)


def _qkv_weight(w_qkv):
    n_cols = w_qkv.shape[1]
    return pl.pallas_call(
        _qkv_weight_kernel,
        out_shape=jax.ShapeDtypeStruct((D, n_cols), BF16),
        grid=(n_cols // TN_WPERM,),
        in_specs=[pl.BlockSpec((D, TN_WPERM), lambda j: (0, j))],
        out_specs=pl.BlockSpec((D, TN_WPERM), lambda j: (0, j)),
        compiler_params=_params(1),
        name="qkv_weight",
    )(w_qkv)


def _rope_tables(gain, scale):
    rows = SEQ // GRID_W
    t_row = jnp.repeat(jnp.arange(rows, dtype=F32), GRID_W)
    t_col = jnp.tile(jnp.arange(GRID_W, dtype=F32), rows)
    per_axis = ADH // 2
    inv = ROPE_THETA ** (-jnp.arange(0, per_axis, 2, dtype=F32) / per_axis)
    ang = jnp.concatenate([t_row[:, None] * inv, t_col[:, None] * inv], axis=-1)
    ang = jnp.concatenate([ang, jnp.zeros((TM_QKV, per_axis), F32)], axis=0)
    cos = jnp.concatenate([jnp.cos(ang), jnp.cos(ang)], axis=-1)
    sin = jnp.concatenate([-jnp.sin(ang), jnp.sin(ang)], axis=-1)
    g = _half_split(gain.astype(F32))
    return (g * scale) * cos, (jnp.roll(g, per_axis) * scale) * sin


def kernel(x, c, ctx, c_ctx, w_mod, b_mod, g_mix_pre, g_mix_post, g_ffn_pre, g_ffn_post,
           w_mlstm_in, b_mlstm_gate, g_mlstm_head, w_mlstm_out,
           w_attn_qkv, g_attn_q, g_attn_k, w_attn_out, w_ffn_in, w_ffn_out):
    assert x.shape == (BATCH, SEQ, D) and ctx.shape == (BATCH, CTX, D)
    xl = x.reshape(N_LAT, D)
    xc = ctx.reshape(N_CTX, D)
    row = lambda a: a.reshape(1, -1)

    c_all = jnp.concatenate([c, c_ctx[None, :], jnp.zeros((MOD_ROWS - BATCH - 1, D), F32)], axis=0)
    mod = _modulation(c_all, w_mod, b_mod).reshape(2, MOD_ROWS, 1, N_MOD * D)

    w_main = w_mlstm_in[0].astype(BF16)
    w_gate = jnp.pad(w_main[:, M_MAIN:], ((0, 0), (0, GATE_PAD - M_GATES)))
    b_gate = jnp.pad(b_mlstm_gate[0], (0, GATE_PAD - M_GATES)).reshape(1, GATE_PAD)
    p, gpre = _mlstm_inproj(xl, xc, row(g_mix_pre[0]), mod[0], w_main, w_gate)
    gc, gr = _gateprep(gpre, b_gate)
    hf, hb = _mlstm_scan(p, gc, gr)
    x_all = _mlstm_out(hf, hb, p, xl, xc, row(g_mlstm_head[0]), row(g_mix_post[0]), mod[0],
                       w_mlstm_out[0].astype(BF16))
    w_ffn_in16 = w_ffn_in.astype(BF16)
    w_ffn_out16 = w_ffn_out.astype(BF16)
    x_all = _ffn(x_all, row(g_ffn_pre[0]), row(g_ffn_post[0]), mod[0], w_ffn_in16, w_ffn_out16, 0)

    w_qkv = _qkv_weight(w_attn_qkv[0])
    a_q, b_q = _rope_tables(g_attn_q[0], A_SCALE * LOG2E)
    a_k, b_k = _rope_tables(g_attn_k[0], 1.0)
    q, k, vt = _attn_qkv(x_all, row(g_mix_pre[1]), mod[1], a_q, b_q, a_k, b_k, w_qkv)
    o = _attention(q, k, vt)
    x_lat = _attn_out(o, x_all, row(g_mix_post[1]), mod[1], w_attn_out[0].astype(BF16))
    x_lat = _ffn(x_lat, row(g_ffn_pre[1]), row(g_ffn_post[1]), mod[1], w_ffn_in16, w_ffn_out16, 1)
    return x_lat.reshape(BATCH, SEQ, D)
```

```python
import functools
import math

import jax
import jax.numpy as jnp
from jax import lax
from jax.experimental import pallas as pl
from jax.experimental.pallas import tpu as pltpu

F32 = jnp.float32
BF16 = jnp.bfloat16

D = 2048
BATCH = 4
SEQ = 4096
CTX = 256
N_LAT = BATCH * SEQ
N_CTX = BATCH * CTX
N_ALL = N_LAT + N_CTX
N_MOD = 6
EPS = 1e-6
MOD_ROWS = 8

MH = 8
MDK = 128
MDV = 256
MQK = MH * MDK
MV = MH * MDV
M_MAIN = 2 * MQK + 2 * MV
M_GATES = 4 * MH
GATE_PAD = 128
SOFTCAP = 15.0
M_SCALE_LOG2 = -0.5 * math.log2(MDK)
LANES = 128

AH = 16
AKV = 4
ADH = 128
AG = AH // AKV
A_SCALE = ADH ** -0.5
LOG2E = math.log2(math.e)
VT_ONES = 16
VT_ROWS = ADH + VT_ONES
GRID_W = 64
ROPE_THETA = 10000.0

FFN = 5632

VMEM_LIMIT = 56 * 1024 * 1024
TM_PROJ = 512
TM_QKV = 1024
TN_INPROJ = 2048
TN_QKV = 1024
TM_OUT = 256
TM_ATT_OUT = 2 * TM_OUT
TM_MLSTM_OUT = 2 * TM_OUT
TM_FFN = 1024
TF_FFN = 512
TN_MOD = 1024
TN_WPERM = 512
L_SCAN = 256
GATEPREP_CHUNKS = 4
TQ_ATT = 1024
TK_ATT = 1024

_NT = (((1,), (1,)), ((), ()))
_TN = (((0,), (0,)), ((), ()))


def _params(n_axes):
    return pltpu.CompilerParams(
        dimension_semantics=("arbitrary",) * n_axes, vmem_limit_bytes=VMEM_LIMIT)


def _mod_row(i, tm):
    return jnp.where(i < N_LAT // tm, i // (SEQ // tm), BATCH)


def _mod_spec(tm, chunk):
    return pl.BlockSpec((None, 1, D), lambda i, j: (_mod_row(i, tm), 0, chunk))


def _adaln(x, gain, shift, scale):
    r = lax.rsqrt(jnp.mean(x * x, axis=-1, keepdims=True) + EPS)
    return (x * r) * (gain * (1.0 + scale)) + shift


def _post_norm_residual(x, y, gain, gate):
    r = lax.rsqrt(jnp.mean(y * y, axis=-1, keepdims=True) + EPS)
    return x + (y * r) * (gain * gate)


def _sigmoid(z):
    return 1.0 / (1.0 + jnp.exp2(z * (-LOG2E)))


def _mod_kernel(c_ref, w_ref, b_ref, o_ref):
    c = c_ref[...]
    cond = (c * _sigmoid(c)).astype(BF16)
    o_ref[...] = jnp.dot(cond, w_ref[...].astype(BF16), preferred_element_type=F32) + b_ref[...]


def _modulation(c_all, w_mod, b_mod):
    depth = w_mod.shape[0]
    return pl.pallas_call(
        _mod_kernel,
        out_shape=jax.ShapeDtypeStruct((depth, MOD_ROWS, N_MOD * D), F32),
        grid=(depth, N_MOD * D // TN_MOD),
        in_specs=[
            pl.BlockSpec((MOD_ROWS, D), lambda l, j: (0, 0)),
            pl.BlockSpec((None, D, TN_MOD), lambda l, j: (l, 0, j)),
            pl.BlockSpec((None, 1, TN_MOD), lambda l, j: (l, 0, j)),
        ],
        out_specs=pl.BlockSpec((None, MOD_ROWS, TN_MOD), lambda l, j: (l, 0, j)),
        compiler_params=_params(2),
        name="modulation",
    )(c_all, w_mod, b_mod.reshape(depth, 1, N_MOD * D))


def _inproj_kernel(xl_ref, xc_ref, g_ref, sh_ref, sc_ref, w_ref, wg_ref, gh_ref, p_ref, gpre_ref,
                   h_scr):
    i = pl.program_id(0)
    j = pl.program_id(1)
    o_tile = (2 * MQK + MV) // TN_INPROJ

    @pl.when(j == 0)
    def _():
        is_latent = i < N_LAT // TM_PROJ
        half = TM_PROJ // 2
        for r in range(0, TM_PROJ, half):
            rows = slice(r, r + half)
            x = jnp.where(is_latent, xl_ref[rows, :], xc_ref[rows, :])
            h = _adaln(x, g_ref[...], sh_ref[...], sc_ref[...]).astype(BF16)
            h_scr[rows, :] = h
            gpre_ref[rows, :] = jnp.dot(h, wg_ref[...], preferred_element_type=F32)
            p_ref[rows, :] = jnp.dot(h, w_ref[...], preferred_element_type=F32).astype(BF16)

    @pl.when((j > 0) & (j < o_tile))
    def _():
        p_ref[...] = jnp.dot(h_scr[...], w_ref[...], preferred_element_type=F32).astype(BF16)

    @pl.when(j == o_tile)
    def _():
        o = jnp.dot(h_scr[...], w_ref[...], preferred_element_type=F32)
        p_ref[...] = (_sigmoid(o) * gh_ref[...]).astype(BF16)


def _two_source_specs(tm):
    n_lat = N_LAT // tm
    return [
        pl.BlockSpec((tm, D), lambda i, *_: (jnp.minimum(i, n_lat - 1), 0)),
        pl.BlockSpec((tm, D), lambda i, *_: (jnp.maximum(i - n_lat, 0), 0)),
    ]


def _mlstm_inproj(xl, xc, gain, mod, w_main, w_gate, g_head):
    tm, tn = TM_PROJ, TN_INPROJ
    assert tn == MV and (2 * MQK + MV) % tn == 0
    vec = pl.BlockSpec((1, D), lambda i, j: (0, 0))
    return pl.pallas_call(
        _inproj_kernel,
        out_shape=(jax.ShapeDtypeStruct((N_ALL, M_MAIN), BF16),
                   jax.ShapeDtypeStruct((N_ALL, GATE_PAD), F32)),
        grid=(N_ALL // tm, M_MAIN // tn),
        in_specs=_two_source_specs(tm) + [
            vec, _mod_spec(tm, 0), _mod_spec(tm, 1),
            pl.BlockSpec((D, tn), lambda i, j: (0, j)),
            pl.BlockSpec((D, GATE_PAD), lambda i, j: (0, 0)),
            pl.BlockSpec((1, MV), lambda i, j: (0, 0)),
        ],
        out_specs=(pl.BlockSpec((tm, tn), lambda i, j: (i, j)),
                   pl.BlockSpec((tm, GATE_PAD), lambda i, j: (i, 0))),
        scratch_shapes=[pltpu.VMEM((tm, D), BF16)],
        compiler_params=_params(2),
        name="mlstm_inproj",
    )(xl, xc, gain, mod, mod, w_main, w_gate, g_head)


def _gateprep_kernel(gpre_ref, b_ref, gc_ref, gr_ref):
    for c in range(GATEPREP_CHUNKS):
        rows = slice(c * L_SCAN, (c + 1) * L_SCAN)
        _gateprep_chunk(gpre_ref.at[rows, :], b_ref, gc_ref.at[rows, :], gr_ref.at[:, rows])


def _gateprep_chunk(gpre_ref, b_ref, gc_ref, gr_ref):
    L = gpre_ref.shape[0]
    z = gpre_ref[...] + b_ref[...]
    a = SOFTCAP * jnp.tanh(z * (1.0 / SOFTCAP))
    logsig = jnp.minimum(a, 0.0) - jnp.log(1.0 + jnp.exp(-jnp.abs(a)))
    row = lax.broadcasted_iota(jnp.int32, (L, GATE_PAD), 0)
    lane = lax.broadcasted_iota(jnp.int32, (L, GATE_PAD), 1)

    def scans(x, op, fill):
        pre, suf = x, x
        s = 1
        while s < L:
            pre = op(pre, jnp.where(row >= s, pltpu.roll(pre, s, 0), fill))
            suf = op(suf, jnp.where(row < L - s, pltpu.roll(suf, L - s, 0), fill))
            s *= 2
        return pre, suf

    fwd_lanes = lane < 2 * MH
    b_pre, b_suf = scans(logsig, jnp.add, 0.0)
    b = jnp.where(fwd_lanes, b_pre, b_suf)
    r = a - pltpu.roll(b, GATE_PAD - MH, 1)
    c_pre, c_suf = scans(r, jnp.maximum, -jnp.inf)
    cmax = jnp.where(fwd_lanes, c_pre, c_suf)
    is_gate_lane = (lane < MH) | ((lane >= 2 * MH) & (lane < 3 * MH))
    gc_ref[...] = jnp.where(is_gate_lane, cmax, b) * LOG2E
    gr_ref[...] = (r * LOG2E).T[:M_GATES, :]


def _gateprep(gpre, bias):
    L = GATEPREP_CHUNKS * L_SCAN
    return pl.pallas_call(
        _gateprep_kernel,
        out_shape=(jax.ShapeDtypeStruct((N_ALL, GATE_PAD), F32),
                   jax.ShapeDtypeStruct((M_GATES, N_ALL), F32)),
        grid=(N_ALL // L,),
        in_specs=[pl.BlockSpec((L, GATE_PAD), lambda i: (i, 0)),
                  pl.BlockSpec((1, GATE_PAD), lambda i: (0, 0))],
        out_specs=(pl.BlockSpec((L, GATE_PAD), lambda i: (i, 0)),
                   pl.BlockSpec((M_GATES, L), lambda i: (0, i))),
        compiler_params=_params(1),
        name="mlstm_gateprep",
    )(gpre, bias)


def _scan_unit(q, k, vx, r_row, cmax_col, b_col, mask, end, cx_ref, m_ref, h_out):
    L = q.shape[0]
    m = m_ref[0:1, 0:1]
    u = jnp.maximum(m, cmax_col)
    u_b = jnp.broadcast_to(u - M_SCALE_LOG2, (L, L))
    dmat = jnp.where(mask, jnp.exp2(r_row - u_b), 0.0)
    sm = (lax.dot_general(q, k, _NT, preferred_element_type=F32) * dmat).astype(BF16)
    u_end = u[end:end + 1, :]
    w_row = jnp.exp2(r_row - u_end)
    kw = (k.astype(F32).T * w_row).astype(BF16)
    both = jnp.dot(jnp.concatenate([sm, kw], axis=0), vx, preferred_element_type=F32)
    intra = both[:L, :]
    cx = cx_ref[...]
    inter = jnp.dot(q, cx.astype(BF16), preferred_element_type=F32)
    u_rep = u_b[:, :LANES]
    w_inter = jnp.exp2(m - u_rep)
    den = intra[:, MDV:] + w_inter * inter[:, MDV:]
    floor = jnp.exp2(-(jnp.broadcast_to(b_col, (L, LANES)) + u_rep + M_SCALE_LOG2))
    inv = 1.0 / jnp.maximum(jnp.abs(den), floor)
    for t in range(MDV // LANES):
        cols = slice(t * LANES, (t + 1) * LANES)
        h_out[:, cols] = ((intra[:, cols] + w_inter * inter[:, cols]) * inv).astype(h_out.dtype)

    cx_ref[...] = jnp.exp2(m - u_end) * cx + both[L:, :]
    m_ref[...] = jnp.broadcast_to(b_col[end:end + 1, :] + u_end, m_ref.shape)


def _scan_kernel(qf, kf, vf, gcf, grf, qb, kb, vb, gcb, grb, hf_ref, hb_ref, cx_scr, m_scr):
    L = qf.shape[0]

    @pl.when(pl.program_id(1) == 0)
    def _():
        cx_scr[...] = jnp.zeros_like(cx_scr)
        m_scr[...] = jnp.zeros_like(m_scr)

    row = lax.broadcasted_iota(jnp.int32, (L, L), 0)
    col = lax.broadcasted_iota(jnp.int32, (L, L), 1)
    ones = jnp.ones((L, LANES), BF16)
    directions = (
        (qf, kf, vf, gcf, grf, hf_ref, col <= row, 0, L - 1),
        (qb, kb, vb, gcb, grb, hb_ref, col >= row, 2 * MH, 0),
    )
    for d, (q_ref, k_ref, v_ref, gc_ref, gr_ref, h_ref, mask, goff, end) in enumerate(directions):
        for h in range(MH):
            ig, fg = goff + h, goff + MH + h
            _scan_unit(
                q_ref[:, h * MDK:(h + 1) * MDK], k_ref[:, h * MDK:(h + 1) * MDK],
                jnp.concatenate([v_ref[:, h * MDV:(h + 1) * MDV], ones], axis=1),
                gr_ref[ig:ig + 1, :], gc_ref[:, ig:ig + 1], gc_ref[:, fg:fg + 1],
                mask, end, cx_scr.at[d * MH + h], m_scr.at[d * MH + h],
                h_ref.at[:, h * MDV:(h + 1) * MDV])


def _mlstm_scan(p, gc, gr):
    L = L_SCAN
    n_ctx_chunks = CTX // L
    n_lat_chunks = SEQ // L
    steps = n_ctx_chunks + n_lat_chunks
    ctx0 = N_LAT // L

    def fwd(b, s):
        return jnp.where(s < n_ctx_chunks, ctx0 + b * n_ctx_chunks + s,
                         b * n_lat_chunks + s - n_ctx_chunks)

    def bwd(b, s):
        return jnp.where(s < n_ctx_chunks, ctx0 + b * n_ctx_chunks + (n_ctx_chunks - 1 - s),
                         b * n_lat_chunks + (n_lat_chunks - 1 - (s - n_ctx_chunks)))

    def specs(idx):
        return [
            pl.BlockSpec((L, MQK), lambda b, s: (idx(b, s), 0)),
            pl.BlockSpec((L, MQK), lambda b, s: (idx(b, s), 1)),
            pl.BlockSpec((L, MV), lambda b, s: (idx(b, s), 1)),
            pl.BlockSpec((L, GATE_PAD), lambda b, s: (idx(b, s), 0)),
            pl.BlockSpec((M_GATES, L), lambda b, s: (0, idx(b, s))),
        ]

    return pl.pallas_call(
        _scan_kernel,
        out_shape=(jax.ShapeDtypeStruct((N_ALL, MV), BF16),
                   jax.ShapeDtypeStruct((N_ALL, MV), BF16)),
        grid=(BATCH, steps),
        in_specs=specs(fwd) + specs(bwd),
        out_specs=(pl.BlockSpec((L, MV), lambda b, s: (fwd(b, s), 0)),
                   pl.BlockSpec((L, MV), lambda b, s: (bwd(b, s), 0))),
        scratch_shapes=[pltpu.VMEM((2 * MH, MDK, MDV + LANES), F32),
                        pltpu.VMEM((2 * MH, 8, LANES), F32)],
        compiler_params=_params(2),
        name="mlstm_scan",
    )(p, p, p, gc, gr, p, p, p, gc, gr)


def _mlstm_out_kernel(hf_ref, hb_ref, og_ref, xl_ref, xc_ref, gp_ref, gt_ref, w_ref,
                      out_ref, a_scr):
    is_latent = pl.program_id(0) < N_LAT // TM_MLSTM_OUT
    for r0 in range(0, TM_MLSTM_OUT, TM_OUT):
        rows = slice(r0, r0 + TM_OUT)
        for h in range(MH):
            sl = slice(h * MDV, (h + 1) * MDV)
            hh = hf_ref[rows, sl].astype(F32) + hb_ref[rows, sl].astype(F32)
            r = lax.rsqrt(jnp.mean(hh * hh, axis=-1, keepdims=True) + EPS)
            a_scr[rows, sl] = ((hh * r) * og_ref[rows, sl].astype(F32)).astype(BF16)
        y = jnp.dot(a_scr[rows, :], w_ref[...], preferred_element_type=F32)
        x = jnp.where(is_latent, xl_ref[rows, :], xc_ref[rows, :])
        out_ref[rows, :] = _post_norm_residual(x, y, gp_ref[...], gt_ref[...])


def _mlstm_out(hf, hb, p, xl, xc, g_post, mod, w_out):
    tm = TM_MLSTM_OUT
    vec = pl.BlockSpec((1, D), lambda i: (0, 0))
    return pl.pallas_call(
        _mlstm_out_kernel,
        out_shape=jax.ShapeDtypeStruct((N_ALL, D), F32),
        grid=(N_ALL // tm,),
        in_specs=[
            pl.BlockSpec((tm, MV), lambda i: (i, 0)),
            pl.BlockSpec((tm, MV), lambda i: (i, 0)),
            pl.BlockSpec((tm, MV), lambda i: (i, 2)),
        ] + _two_source_specs(tm) + [
            vec,
            pl.BlockSpec((None, 1, D), lambda i: (_mod_row(i, tm), 0, 2)),
            pl.BlockSpec((MV, D), lambda i: (0, 0), pipeline_mode=pl.Buffered(1)),
        ],
        out_specs=pl.BlockSpec((tm, D), lambda i: (i, 0)),
        scratch_shapes=[pltpu.VMEM((tm, MV), BF16)],
        compiler_params=_params(1),
        name="mlstm_out",
    )(hf, hb, p, xl, xc, g_post, mod, w_out)


def _ffn_kernel(x_ref, g_ref, sh_ref, sc_ref, gt_ref, gp_ref, wg_ref, wu_ref, wo_ref, out_ref,
                h_scr):
    f = pl.program_id(1)
    last_f = pl.num_programs(1) - 1
    half = TM_FFN // 2

    def step(first, last):
        for r in range(0, TM_FFN, half):
            rows = slice(r, r + half)
            if first:
                h_scr[rows, :] = _adaln(x_ref[rows, :], g_ref[...], sh_ref[...],
                                        sc_ref[...]).astype(BF16)
            h = h_scr[rows, :]
            gate = jnp.dot(h, wg_ref[...], preferred_element_type=F32)
            up = jnp.dot(h, wu_ref[...], preferred_element_type=F32)
            act = ((gate * _sigmoid(gate)) * up).astype(BF16)
            y = jnp.dot(act, wo_ref[...], preferred_element_type=F32)
            if not first:
                y = out_ref[rows, :] + y
            if last:
                y = _post_norm_residual(x_ref[rows, :], y, gp_ref[...], gt_ref[...])
            out_ref[rows, :] = y

    pl.when(f == 0)(functools.partial(step, True, False))
    pl.when((f > 0) & (f < last_f))(functools.partial(step, False, False))
    pl.when(f == last_f)(functools.partial(step, False, True))


def _ffn(x_rows, g_pre, g_post, mod, w_in, w_out, layer):
    n_rows = x_rows.shape[0]
    tm, tf = TM_FFN, TF_FFN
    nf = FFN // tf
    assert nf >= 2
    vec = pl.BlockSpec((1, D), lambda i, f: (0, 0))
    return pl.pallas_call(
        _ffn_kernel,
        out_shape=jax.ShapeDtypeStruct((n_rows, D), F32),
        grid=(n_rows // tm, nf),
        in_specs=[
            pl.BlockSpec((tm, D), lambda i, f: (i, 0)),
            vec, _mod_spec(tm, 3), _mod_spec(tm, 4), _mod_spec(tm, 5), vec,
            pl.BlockSpec((None, D, tf), lambda i, f: (layer, 0, f)),
            pl.BlockSpec((None, D, tf), lambda i, f: (layer, 0, nf + f)),
            pl.BlockSpec((None, tf, D), lambda i, f: (layer, f, 0)),
        ],
        out_specs=pl.BlockSpec((tm, D), lambda i, f: (i, 0)),
        scratch_shapes=[pltpu.VMEM((tm, D), BF16)],
        compiler_params=_params(2),
        name="ffn",
    )(x_rows, g_pre, mod, mod, mod, g_post, w_in, w_in, w_out)


def _norm_rope_heads(y_ref, out_ref, n_cols, a, b):
    for c in range(0, n_cols, ADH):
        yy = y_ref[:, c:c + ADH]
        r = lax.rsqrt(jnp.mean(yy * yy, axis=-1, keepdims=True) + EPS)
        out_ref[:, c:c + ADH] = ((yy * a + pltpu.roll(yy, ADH // 2, 1) * b) * r).astype(BF16)


def _qkv_kernel(x_ref, g_ref, sh_ref, sc_ref, aq_ref, bq_ref, ak_ref, bk_ref, w_ref,
                q_ref, k_ref, vt_ref, h_scr, y_scr):
    j = pl.program_id(1)
    n_q = AH * ADH // TN_QKV
    kv_cols = AKV * ADH

    def project(jj):
        y_scr[jj % 2] = jnp.dot(h_scr[...], w_ref[...], preferred_element_type=F32)

    def finish(jj):
        y_ref = y_scr.at[jj % 2]
        if jj < n_q:
            _norm_rope_heads(y_ref, q_ref, TN_QKV, aq_ref[...], bq_ref[...])
        else:
            _norm_rope_heads(y_ref, k_ref, kv_cols, ak_ref[...], bk_ref[...])
            vt = y_ref[:, kv_cols:].T
            ones = jnp.ones((VT_ONES, vt.shape[1]), BF16)
            for hd in range(AKV):
                vt_ref[hd * VT_ROWS:hd * VT_ROWS + ADH, :] = (
                    vt[hd * ADH:(hd + 1) * ADH, :].astype(BF16))
                vt_ref[hd * VT_ROWS + ADH:(hd + 1) * VT_ROWS, :] = ones

    def step(jj):
        if jj == 0:
            h_scr[...] = _adaln(x_ref[...], g_ref[...], sh_ref[...], sc_ref[...]).astype(BF16)
        if jj <= n_q:
            project(jj)
        if jj >= 1:
            finish(jj - 1)

    for jj in range(n_q + 2):
        pl.when(j == jj)(functools.partial(step, jj))


def _attn_qkv(x_all, gain, mod, a_q, b_q, a_k, b_k, w_qkv):
    tm, tn = TM_QKV, TN_QKV
    n_q = AH * ADH // tn
    n_lat = N_LAT // tm
    per_seq = SEQ // tm
    kv_cols = AKV * ADH
    assert tn == 2 * kv_cols
    vec = pl.BlockSpec((1, D), lambda i, j: (0, 0))
    rope = pl.BlockSpec((tm, ADH), lambda i, j: (jnp.where(i < n_lat, i % per_seq, per_seq), 0))
    return pl.pallas_call(
        _qkv_kernel,
        out_shape=(jax.ShapeDtypeStruct((N_ALL, AH * ADH), BF16),
                   jax.ShapeDtypeStruct((N_ALL, kv_cols), BF16),
                   jax.ShapeDtypeStruct((AKV * VT_ROWS, N_ALL), BF16)),
        grid=(N_ALL // tm, n_q + 2),
        in_specs=[
            pl.BlockSpec((tm, D), lambda i, j: (i, 0)),
            vec, _mod_spec(tm, 0), _mod_spec(tm, 1), rope, rope, rope, rope,
            pl.BlockSpec((D, tn), lambda i, j: (0, jnp.minimum(j, n_q))),
        ],
        out_specs=(pl.BlockSpec((tm, tn), lambda i, j: (i, jnp.clip(j - 1, 0, n_q - 1))),
                   pl.BlockSpec((tm, kv_cols), lambda i, j: (i, 0)),
                   pl.BlockSpec((AKV * VT_ROWS, tm), lambda i, j: (0, i))),
        scratch_shapes=[pltpu.VMEM((tm, D), BF16), pltpu.VMEM((2, tm, tn), F32)],
        compiler_params=_params(2),
        name="attn_qkv",
    )(x_all, gain, mod, mod, a_q, b_q, a_k, b_k, w_qkv)


def _attn_kernel(q_ref, kl_ref, kc_ref, vtl_ref, vtc_ref, o_ref):
    tq = q_ref.shape[0]
    nq = AG * tq
    q = jnp.concatenate([q_ref[:, g * ADH:(g + 1) * ADH] for g in range(AG)], axis=0)

    def block(k, vt, m, acc):
        st = lax.dot_general(k, q, _NT, preferred_element_type=F32)
        m_new = jnp.maximum(m, jnp.max(st, axis=0, keepdims=True))
        p = jnp.exp2(st - m_new).astype(BF16)
        acc = jnp.exp2(m - m_new) * acc + jnp.dot(vt, p, preferred_element_type=F32)
        return m_new, acc

    m = jnp.full((1, nq), -jnp.inf, F32)
    acc = jnp.zeros((VT_ROWS, nq), F32)
    for c in range(SEQ // TK_ATT):
        m, acc = block(kl_ref[c * TK_ATT:(c + 1) * TK_ATT, :],
                       vtl_ref[:, c * TK_ATT:(c + 1) * TK_ATT], m, acc)
    m, acc = block(kc_ref[...], vtc_ref[...], m, acc)
    out = (acc[:ADH, :] * (1.0 / acc[ADH:ADH + 1, :])).T
    for g in range(AG):
        o_ref[:, g * ADH:(g + 1) * ADH] = out[g * tq:(g + 1) * tq, :].astype(BF16)


def _attention(q, k, vt):
    tq = TQ_ATT
    per_seq = SEQ // tq
    return pl.pallas_call(
        _attn_kernel,
        out_shape=jax.ShapeDtypeStruct((N_LAT, AH * ADH), BF16),
        grid=(BATCH, AKV, per_seq),
        in_specs=[
            pl.BlockSpec((tq, AG * ADH), lambda b, h, i: (b * per_seq + i, h)),
            pl.BlockSpec((SEQ, ADH), lambda b, h, i: (b, h)),
            pl.BlockSpec((CTX, ADH), lambda b, h, i: (N_LAT // CTX + b, h)),
            pl.BlockSpec((VT_ROWS, SEQ), lambda b, h, i: (h, b)),
            pl.BlockSpec((VT_ROWS, CTX), lambda b, h, i: (h, N_LAT // CTX + b)),
        ],
        out_specs=pl.BlockSpec((tq, AG * ADH), lambda b, h, i: (b * per_seq + i, h)),
        compiler_params=_params(3),
        name="attention",
    )(q, k, k, vt, vt)


def _attn_out_kernel(o_ref, x_ref, gp_ref, gt_ref, w_ref, out_ref):
    for r in range(0, TM_ATT_OUT, TM_OUT):
        rows = slice(r, r + TM_OUT)
        y = jnp.dot(o_ref[rows, :], w_ref[...], preferred_element_type=F32)
        out_ref[rows, :] = _post_norm_residual(x_ref[rows, :], y, gp_ref[...], gt_ref[...])


def _attn_out(o, x_all, g_post, mod, w_out):
    tm = TM_ATT_OUT
    return pl.pallas_call(
        _attn_out_kernel,
        out_shape=jax.ShapeDtypeStruct((N_LAT, D), F32),
        grid=(N_LAT // tm,),
        in_specs=[
            pl.BlockSpec((tm, D), lambda i: (i, 0)),
            pl.BlockSpec((tm, D), lambda i: (i, 0)),
            pl.BlockSpec((1, D), lambda i: (0, 0)),
            pl.BlockSpec((None, 1, D), lambda i: (_mod_row(i, tm), 0, 2)),
            pl.BlockSpec((D, D), lambda i: (0, 0)),
        ],
        out_specs=pl.BlockSpec((tm, D), lambda i: (i, 0)),
        compiler_params=_params(1),
        name="attn_out",
    )(o, x_all, g_post, mod, w_out)


def _half_split(a):
    lead = a.shape[:-1]
    n = a.shape[-1] // ADH
    a = a.reshape(*lead, n, ADH // 2, 2)
    return jnp.swapaxes(a, -1, -2).reshape(*lead, n * ADH)


def _qkv_weight_kernel(w_ref, o_ref):
    first_head = pl.program_id(0) * (TN_WPERM // ADH)
    src = lax.broadcasted_iota(jnp.int32, (ADH, ADH), 0)
    dst = lax.broadcasted_iota(jnp.int32, (ADH, ADH), 1)
    half = ADH // 2
    split_src = jnp.where(dst < half, 2 * dst, 2 * (dst - half) + 1)
    for h in range(TN_WPERM // ADH):
        is_qk = first_head + h < AH + AKV
        perm = (src == jnp.where(is_qk, split_src, dst)).astype(BF16)
        cols = slice(h * ADH, (h + 1) * ADH)
        o_ref[:, cols] = jnp.dot(w_ref[:, cols].astype(BF16), perm,
                                 preferred_element_type=F32).astype(BF16)


def _qkv_weight(w_qkv):
    n_cols = w_qkv.shape[1]
    return pl.pallas_call(
        _qkv_weight_kernel,
        out_shape=jax.ShapeDtypeStruct((D, n_cols), BF16),
        grid=(n_cols // TN_WPERM,),
        in_specs=[pl.BlockSpec((D, TN_WPERM), lambda j: (0, j))],
        out_specs=pl.BlockSpec((D, TN_WPERM), lambda j: (0, j)),
        compiler_params=_params(1),
        name="qkv_weight",
    )(w_qkv)


def _rope_tables(gain, scale):
    rows = SEQ // GRID_W
    t_row = jnp.repeat(jnp.arange(rows, dtype=F32), GRID_W)
    t_col = jnp.tile(jnp.arange(GRID_W, dtype=F32), rows)
    per_axis = ADH // 2
    inv = ROPE_THETA ** (-jnp.arange(0, per_axis, 2, dtype=F32) / per_axis)
    ang = jnp.concatenate([t_row[:, None] * inv, t_col[:, None] * inv], axis=-1)
    ang = jnp.concatenate([ang, jnp.zeros((TM_QKV, per_axis), F32)], axis=0)
    cos = jnp.concatenate([jnp.cos(ang), jnp.cos(ang)], axis=-1)
    sin = jnp.concatenate([-jnp.sin(ang), jnp.sin(ang)], axis=-1)
    g = _half_split(gain.astype(F32))
    return (g * scale) * cos, (jnp.roll(g, per_axis) * scale) * sin


def kernel(x, c, ctx, c_ctx, w_mod, b_mod, g_mix_pre, g_mix_post, g_ffn_pre, g_ffn_post,
           w_mlstm_in, b_mlstm_gate, g_mlstm_head, w_mlstm_out,
           w_attn_qkv, g_attn_q, g_attn_k, w_attn_out, w_ffn_in, w_ffn_out):
    assert x.shape == (BATCH, SEQ, D) and ctx.shape == (BATCH, CTX, D)
    xl = x.reshape(N_LAT, D)
    xc = ctx.reshape(N_CTX, D)
    row = lambda a: a.reshape(1, -1)

    c_all = jnp.concatenate([c, c_ctx[None, :], jnp.zeros((MOD_ROWS - BATCH - 1, D), F32)], axis=0)
    mod = _modulation(c_all, w_mod, b_mod).reshape(2, MOD_ROWS, 1, N_MOD * D)

    w_main = w_mlstm_in[0].astype(BF16)
    w_gate = jnp.pad(w_main[:, M_MAIN:], ((0, 0), (0, GATE_PAD - M_GATES)))
    b_gate = jnp.pad(b_mlstm_gate[0], (0, GATE_PAD - M_GATES)).reshape(1, GATE_PAD)
    p, gpre = _mlstm_inproj(xl, xc, row(g_mix_pre[0]), mod[0], w_main, w_gate,
                            row(g_mlstm_head[0]))
    gc, gr = _gateprep(gpre, b_gate)
    hf, hb = _mlstm_scan(p, gc, gr)
    x_all = _mlstm_out(hf, hb, p, xl, xc, row(g_mix_post[0]), mod[0],
                       w_mlstm_out[0].astype(BF16))
    w_ffn_in16 = w_ffn_in.astype(BF16)
    w_ffn_out16 = w_ffn_out.astype(BF16)
    x_all = _ffn(x_all, row(g_ffn_pre[0]), row(g_ffn_post[0]), mod[0], w_ffn_in16, w_ffn_out16, 0)

    w_qkv = _qkv_weight(w_attn_qkv[0])
    a_q, b_q = _rope_tables(g_attn_q[0], A_SCALE * LOG2E)
    a_k, b_k = _rope_tables(g_attn_k[0], 1.0)
    q, k, vt = _attn_qkv(x_all, row(g_mix_pre[1]), mod[1], a_q, b_q, a_k, b_k, w_qkv)
    o = _attention(q, k, vt)
    x_lat = _attn_out(o, x_all, row(g_mix_post[1]), mod[1], w_attn_out[0].astype(BF16))
    x_lat = _ffn(x_lat, row(g_ffn_pre[1]), row(g_ffn_post[1]), mod[1], w_ffn_in16, w_ffn_out16, 1)
    return x_lat.reshape(BATCH, SEQ, D)
```

```python
import functools
import math

import jax
import jax.numpy as jnp
from jax import lax
from jax.experimental import pallas as pl
from jax.experimental.pallas import tpu as pltpu

F32 = jnp.float32
BF16 = jnp.bfloat16

D = 2048
BATCH = 4
SEQ = 4096
CTX = 256
N_LAT = BATCH * SEQ
N_CTX = BATCH * CTX
N_ALL = N_LAT + N_CTX
N_MOD = 6
EPS = 1e-6
MOD_ROWS = 8

MH = 8
MDK = 128
MDV = 256
MQK = MH * MDK
MV = MH * MDV
M_MAIN = 2 * MQK + 2 * MV
M_GATES = 4 * MH
GATE_PAD = 128
SOFTCAP = 15.0
M_SCALE_LOG2 = -0.5 * math.log2(MDK)
LANES = 128

AH = 16
AKV = 4
ADH = 128
AG = AH // AKV
A_SCALE = ADH ** -0.5
LOG2E = math.log2(math.e)
VT_ONES = 16
VT_ROWS = ADH + VT_ONES
GRID_W = 64
ROPE_THETA = 10000.0

FFN = 5632

VMEM_LIMIT = 56 * 1024 * 1024
TM_PROJ = 512
TM_QKV = 1024
TN_INPROJ = 2048
TN_QKV = 1024
TM_OUT = 256
TM_ATT_OUT = 2 * TM_OUT
TM_MLSTM_OUT = 2 * TM_OUT
TM_FFN = 1024
TF_FFN = 512
TN_MOD = 1024
TN_WPERM = 512
L_SCAN = 256
GATEPREP_CHUNKS = 4
TQ_ATT = 1024
TK_ATT = 1024

_NT = (((1,), (1,)), ((), ()))
_TN = (((0,), (0,)), ((), ()))


def _params(n_axes):
    return pltpu.CompilerParams(
        dimension_semantics=("arbitrary",) * n_axes, vmem_limit_bytes=VMEM_LIMIT)


def _mod_row(i, tm):
    return jnp.where(i < N_LAT // tm, i // (SEQ // tm), BATCH)


def _mod_spec(tm, chunk):
    return pl.BlockSpec((None, 1, D), lambda i, j: (_mod_row(i, tm), 0, chunk))


def _adaln(x, gain, shift, scale):
    r = lax.rsqrt(jnp.mean(x * x, axis=-1, keepdims=True) + EPS)
    return (x * r) * (gain * (1.0 + scale)) + shift


def _post_norm_residual(x, y, gain, gate):
    r = lax.rsqrt(jnp.mean(y * y, axis=-1, keepdims=True) + EPS)
    return x + (y * r) * (gain * gate)


def _sigmoid(z):
    return 1.0 / (1.0 + jnp.exp2(z * (-LOG2E)))


def _mod_kernel(c_ref, w_ref, b_ref, o_ref):
    c = c_ref[...]
    cond = (c * _sigmoid(c)).astype(BF16)
    o_ref[...] = jnp.dot(cond, w_ref[...].astype(BF16), preferred_element_type=F32) + b_ref[...]


def _modulation(c_all, w_mod, b_mod):
    depth = w_mod.shape[0]
    return pl.pallas_call(
        _mod_kernel,
        out_shape=jax.ShapeDtypeStruct((depth, MOD_ROWS, N_MOD * D), F32),
        grid=(depth, N_MOD * D // TN_MOD),
        in_specs=[
            pl.BlockSpec((MOD_ROWS, D), lambda l, j: (0, 0)),
            pl.BlockSpec((None, D, TN_MOD), lambda l, j: (l, 0, j)),
            pl.BlockSpec((None, 1, TN_MOD), lambda l, j: (l, 0, j)),
        ],
        out_specs=pl.BlockSpec((None, MOD_ROWS, TN_MOD), lambda l, j: (l, 0, j)),
        compiler_params=_params(2),
        name="modulation",
    )(c_all, w_mod, b_mod.reshape(depth, 1, N_MOD * D))


def _inproj_kernel(xl_ref, xc_ref, g_ref, sh_ref, sc_ref, w_ref, wg_ref, gh_ref, p_ref, gpre_ref,
                   h_scr):
    i = pl.program_id(0)
    j = pl.program_id(1)
    o_tile = (2 * MQK + MV) // TN_INPROJ

    @pl.when(j == 0)
    def _():
        is_latent = i < N_LAT // TM_PROJ
        half = TM_PROJ // 2
        for r in range(0, TM_PROJ, half):
            rows = slice(r, r + half)
            x = jnp.where(is_latent, xl_ref[rows, :], xc_ref[rows, :])
            h = _adaln(x, g_ref[...], sh_ref[...], sc_ref[...]).astype(BF16)
            h_scr[rows, :] = h
            gpre_ref[rows, :] = jnp.dot(h, wg_ref[...], preferred_element_type=F32)
            p_ref[rows, :] = jnp.dot(h, w_ref[...], preferred_element_type=F32).astype(BF16)

    @pl.when((j > 0) & (j < o_tile))
    def _():
        p_ref[...] = jnp.dot(h_scr[...], w_ref[...], preferred_element_type=F32).astype(BF16)

    @pl.when(j == o_tile)
    def _():
        o = jnp.dot(h_scr[...], w_ref[...], preferred_element_type=F32)
        p_ref[...] = (_sigmoid(o) * gh_ref[...]).astype(BF16)


def _two_source_specs(tm):
    n_lat = N_LAT // tm
    return [
        pl.BlockSpec((tm, D), lambda i, *_: (jnp.minimum(i, n_lat - 1), 0)),
        pl.BlockSpec((tm, D), lambda i, *_: (jnp.maximum(i - n_lat, 0), 0)),
    ]


def _mlstm_inproj(xl, xc, gain, mod, w_main, w_gate, g_head):
    tm, tn = TM_PROJ, TN_INPROJ
    assert tn == MV and (2 * MQK + MV) % tn == 0
    vec = pl.BlockSpec((1, D), lambda i, j: (0, 0))
    return pl.pallas_call(
        _inproj_kernel,
        out_shape=(jax.ShapeDtypeStruct((N_ALL, M_MAIN), BF16),
                   jax.ShapeDtypeStruct((N_ALL, GATE_PAD), F32)),
        grid=(N_ALL // tm, M_MAIN // tn),
        in_specs=_two_source_specs(tm) + [
            vec, _mod_spec(tm, 0), _mod_spec(tm, 1),
            pl.BlockSpec((D, tn), lambda i, j: (0, j)),
            pl.BlockSpec((D, GATE_PAD), lambda i, j: (0, 0)),
            pl.BlockSpec((1, MV), lambda i, j: (0, 0)),
        ],
        out_specs=(pl.BlockSpec((tm, tn), lambda i, j: (i, j)),
                   pl.BlockSpec((tm, GATE_PAD), lambda i, j: (i, 0))),
        scratch_shapes=[pltpu.VMEM((tm, D), BF16)],
        compiler_params=_params(2),
        name="mlstm_inproj",
    )(xl, xc, gain, mod, mod, w_main, w_gate, g_head)


def _gateprep_kernel(gpre_ref, b_ref, gc_ref, gr_ref):
    for c in range(GATEPREP_CHUNKS):
        rows = slice(c * L_SCAN, (c + 1) * L_SCAN)
        _gateprep_chunk(gpre_ref.at[rows, :], b_ref, gc_ref.at[rows, :], gr_ref.at[:, rows])


def _gateprep_chunk(gpre_ref, b_ref, gc_ref, gr_ref):
    L = gpre_ref.shape[0]
    z = gpre_ref[...] + b_ref[...]
    a = SOFTCAP * jnp.tanh(z * (1.0 / SOFTCAP))
    logsig = jnp.minimum(a, 0.0) - jnp.log(1.0 + jnp.exp(-jnp.abs(a)))
    row = lax.broadcasted_iota(jnp.int32, (L, GATE_PAD), 0)
    lane = lax.broadcasted_iota(jnp.int32, (L, GATE_PAD), 1)

    def scans(x, op, fill):
        pre, suf = x, x
        s = 1
        while s < L:
            pre = op(pre, jnp.where(row >= s, pltpu.roll(pre, s, 0), fill))
            suf = op(suf, jnp.where(row < L - s, pltpu.roll(suf, L - s, 0), fill))
            s *= 2
        return pre, suf

    fwd_lanes = lane < 2 * MH
    b_pre, b_suf = scans(logsig, jnp.add, 0.0)
    b = jnp.where(fwd_lanes, b_pre, b_suf)
    r = a - pltpu.roll(b, GATE_PAD - MH, 1)
    c_pre, c_suf = scans(r, jnp.maximum, -jnp.inf)
    cmax = jnp.where(fwd_lanes, c_pre, c_suf)
    is_gate_lane = (lane < MH) | ((lane >= 2 * MH) & (lane < 3 * MH))
    gc_ref[...] = jnp.where(is_gate_lane, cmax, b) * LOG2E
    gr_ref[...] = (r * LOG2E).T[:M_GATES, :]


def _gateprep(gpre, bias):
    L = GATEPREP_CHUNKS * L_SCAN
    return pl.pallas_call(
        _gateprep_kernel,
        out_shape=(jax.ShapeDtypeStruct((N_ALL, GATE_PAD), F32),
                   jax.ShapeDtypeStruct((M_GATES, N_ALL), F32)),
        grid=(N_ALL // L,),
        in_specs=[pl.BlockSpec((L, GATE_PAD), lambda i: (i, 0)),
                  pl.BlockSpec((1, GATE_PAD), lambda i: (0, 0))],
        out_specs=(pl.BlockSpec((L, GATE_PAD), lambda i: (i, 0)),
                   pl.BlockSpec((M_GATES, L), lambda i: (0, i))),
        compiler_params=_params(1),
        name="mlstm_gateprep",
    )(gpre, bias)


def _scan_unit(q, k, vx, r_row, cmax_col, b_col, mask, end, cx_ref, m_ref, h_out):
    L = q.shape[0]
    m = m_ref[0:1, 0:1]
    u = jnp.maximum(m, cmax_col)
    u_b = jnp.broadcast_to(u - M_SCALE_LOG2, (L, L))
    dmat = jnp.where(mask, jnp.exp2(r_row - u_b), 0.0)
    sm = (lax.dot_general(q, k, _NT, preferred_element_type=F32) * dmat).astype(BF16)
    u_end = u[end:end + 1, :]
    w_row = jnp.exp2(r_row - u_end)
    kw = (k.astype(F32).T * w_row).astype(BF16)
    both = jnp.dot(jnp.concatenate([sm, kw], axis=0), vx, preferred_element_type=F32)
    intra = both[:L, :]
    cx = cx_ref[...]
    inter = jnp.dot(q, cx.astype(BF16), preferred_element_type=F32)
    u_rep = u_b[:, :LANES]
    w_inter = jnp.exp2(m - u_rep)
    den = intra[:, MDV:] + w_inter * inter[:, MDV:]
    floor = jnp.exp2(-(jnp.broadcast_to(b_col, (L, LANES)) + u_rep + M_SCALE_LOG2))
    inv = 1.0 / jnp.maximum(jnp.abs(den), floor)
    for t in range(MDV // LANES):
        cols = slice(t * LANES, (t + 1) * LANES)
        h_out[:, cols] = ((intra[:, cols] + w_inter * inter[:, cols]) * inv).astype(h_out.dtype)

    cx_ref[...] = jnp.exp2(m - u_end) * cx + both[L:, :]
    m_ref[...] = jnp.broadcast_to(b_col[end:end + 1, :] + u_end, m_ref.shape)


def _scan_kernel(qf, kf, vf, gcf, grf, qb, kb, vb, gcb, grb, hf_ref, hb_ref, cx_scr, m_scr):
    L = qf.shape[0]

    @pl.when(pl.program_id(1) == 0)
    def _():
        cx_scr[...] = jnp.zeros_like(cx_scr)
        m_scr[...] = jnp.zeros_like(m_scr)

    row = lax.broadcasted_iota(jnp.int32, (L, L), 0)
    col = lax.broadcasted_iota(jnp.int32, (L, L), 1)
    ones = jnp.ones((L, LANES), BF16)
    directions = (
        (qf, kf, vf, gcf, grf, hf_ref, col <= row, 0, L - 1),
        (qb, kb, vb, gcb, grb, hb_ref, col >= row, 2 * MH, 0),
    )
    for d, (q_ref, k_ref, v_ref, gc_ref, gr_ref, h_ref, mask, goff, end) in enumerate(directions):
        for h in range(MH):
            ig, fg = goff + h, goff + MH + h
            _scan_unit(
                q_ref[:, h * MDK:(h + 1) * MDK], k_ref[:, h * MDK:(h + 1) * MDK],
                jnp.concatenate([v_ref[:, h * MDV:(h + 1) * MDV], ones], axis=1),
                gr_ref[ig:ig + 1, :], gc_ref[:, ig:ig + 1], gc_ref[:, fg:fg + 1],
                mask, end, cx_scr.at[d * MH + h], m_scr.at[d * MH + h],
                h_ref.at[:, h * MDV:(h + 1) * MDV])


def _mlstm_scan(p, gc, gr):
    L = L_SCAN
    n_ctx_chunks = CTX // L
    n_lat_chunks = SEQ // L
    steps = n_ctx_chunks + n_lat_chunks
    ctx0 = N_LAT // L

    def fwd(b, s):
        return jnp.where(s < n_ctx_chunks, ctx0 + b * n_ctx_chunks + s,
                         b * n_lat_chunks + s - n_ctx_chunks)

    def bwd(b, s):
        return jnp.where(s < n_ctx_chunks, ctx0 + b * n_ctx_chunks + (n_ctx_chunks - 1 - s),
                         b * n_lat_chunks + (n_lat_chunks - 1 - (s - n_ctx_chunks)))

    def specs(idx):
        return [
            pl.BlockSpec((L, MQK), lambda b, s: (idx(b, s), 0)),
            pl.BlockSpec((L, MQK), lambda b, s: (idx(b, s), 1)),
            pl.BlockSpec((L, MV), lambda b, s: (idx(b, s), 1)),
            pl.BlockSpec((L, GATE_PAD), lambda b, s: (idx(b, s), 0)),
            pl.BlockSpec((M_GATES, L), lambda b, s: (0, idx(b, s))),
        ]

    return pl.pallas_call(
        _scan_kernel,
        out_shape=(jax.ShapeDtypeStruct((N_ALL, MV), BF16),
                   jax.ShapeDtypeStruct((N_ALL, MV), BF16)),
        grid=(BATCH, steps),
        in_specs=specs(fwd) + specs(bwd),
        out_specs=(pl.BlockSpec((L, MV), lambda b, s: (fwd(b, s), 0)),
                   pl.BlockSpec((L, MV), lambda b, s: (bwd(b, s), 0))),
        scratch_shapes=[pltpu.VMEM((2 * MH, MDK, MDV + LANES), F32),
                        pltpu.VMEM((2 * MH, 8, LANES), F32)],
        compiler_params=_params(2),
        name="mlstm_scan",
    )(p, p, p, gc, gr, p, p, p, gc, gr)


def _mlstm_out_kernel(hf_ref, hb_ref, og_ref, xl_ref, xc_ref, gp_ref, gt_ref, w_ref,
                      out_ref, a_scr):
    is_latent = pl.program_id(0) < N_LAT // TM_MLSTM_OUT
    for r0 in range(0, TM_MLSTM_OUT, TM_OUT):
        rows = slice(r0, r0 + TM_OUT)
        for h in range(MH):
            sl = slice(h * MDV, (h + 1) * MDV)
            hh = hf_ref[rows, sl].astype(F32) + hb_ref[rows, sl].astype(F32)
            r = lax.rsqrt(jnp.mean(hh * hh, axis=-1, keepdims=True) + EPS)
            a_scr[rows, sl] = ((hh * r) * og_ref[rows, sl].astype(F32)).astype(BF16)
        y = jnp.dot(a_scr[rows, :], w_ref[...], preferred_element_type=F32)
        x = jnp.where(is_latent, xl_ref[rows, :], xc_ref[rows, :])
        out_ref[rows, :] = _post_norm_residual(x, y, gp_ref[...], gt_ref[...])


def _mlstm_out(hf, hb, p, xl, xc, g_post, mod, w_out):
    tm = TM_MLSTM_OUT
    vec = pl.BlockSpec((1, D), lambda i: (0, 0))
    return pl.pallas_call(
        _mlstm_out_kernel,
        out_shape=jax.ShapeDtypeStruct((N_ALL, D), F32),
        grid=(N_ALL // tm,),
        in_specs=[
            pl.BlockSpec((tm, MV), lambda i: (i, 0)),
            pl.BlockSpec((tm, MV), lambda i: (i, 0)),
            pl.BlockSpec((tm, MV), lambda i: (i, 2)),
        ] + _two_source_specs(tm) + [
            vec,
            pl.BlockSpec((None, 1, D), lambda i: (_mod_row(i, tm), 0, 2)),
            pl.BlockSpec((MV, D), lambda i: (0, 0), pipeline_mode=pl.Buffered(1)),
        ],
        out_specs=pl.BlockSpec((tm, D), lambda i: (i, 0)),
        scratch_shapes=[pltpu.VMEM((tm, MV), BF16)],
        compiler_params=_params(1),
        name="mlstm_out",
    )(hf, hb, p, xl, xc, g_post, mod, w_out)


def _ffn_kernel(x_ref, g_ref, sh_ref, sc_ref, gt_ref, gp_ref, wg_ref, wu_ref, wo_ref, out_ref,
                h_scr):
    f = pl.program_id(1)
    last_f = pl.num_programs(1) - 1
    half = TM_FFN // 2

    def step(first, last):
        for r in range(0, TM_FFN, half):
            rows = slice(r, r + half)
            if first:
                h_scr[rows, :] = _adaln(x_ref[rows, :], g_ref[...], sh_ref[...],
                                        sc_ref[...]).astype(BF16)
            h = h_scr[rows, :]
            gate = jnp.dot(h, wg_ref[...], preferred_element_type=F32)
            up = jnp.dot(h, wu_ref[...], preferred_element_type=F32)
            act = ((gate * _sigmoid(gate)) * up).astype(BF16)
            y = jnp.dot(act, wo_ref[...], preferred_element_type=F32)
            if not first:
                y = out_ref[rows, :] + y
            if last:
                y = _post_norm_residual(x_ref[rows, :], y, gp_ref[...], gt_ref[...])
            out_ref[rows, :] = y

    pl.when(f == 0)(functools.partial(step, True, False))
    pl.when((f > 0) & (f < last_f))(functools.partial(step, False, False))
    pl.when(f == last_f)(functools.partial(step, False, True))


def _ffn(x_rows, g_pre, g_post, mod, w_in, w_out, layer):
    n_rows = x_rows.shape[0]
    tm, tf = TM_FFN, TF_FFN
    nf = FFN // tf
    assert nf >= 2
    vec = pl.BlockSpec((1, D), lambda i, f: (0, 0))
    return pl.pallas_call(
        _ffn_kernel,
        out_shape=jax.ShapeDtypeStruct((n_rows, D), F32),
        grid=(n_rows // tm, nf),
        in_specs=[
            pl.BlockSpec((tm, D), lambda i, f: (i, 0)),
            vec, _mod_spec(tm, 3), _mod_spec(tm, 4), _mod_spec(tm, 5), vec,
            pl.BlockSpec((None, D, tf), lambda i, f: (layer, 0, f)),
            pl.BlockSpec((None, D, tf), lambda i, f: (layer, 0, nf + f)),
            pl.BlockSpec((None, tf, D), lambda i, f: (layer, f, 0)),
        ],
        out_specs=pl.BlockSpec((tm, D), lambda i, f: (i, 0)),
        scratch_shapes=[pltpu.VMEM((tm, D), BF16)],
        compiler_params=_params(2),
        name="ffn",
    )(x_rows, g_pre, mod, mod, mod, g_post, w_in, w_in, w_out)


def _norm_rope_heads(y_ref, out_ref, n_cols, a, b):
    for c in range(0, n_cols, ADH):
        yy = y_ref[:, c:c + ADH]
        r = lax.rsqrt(jnp.mean(yy * yy, axis=-1, keepdims=True) + EPS)
        out_ref[:, c:c + ADH] = ((yy * a + pltpu.roll(yy, ADH // 2, 1) * b) * r).astype(BF16)


def _qkv_kernel(x_ref, g_ref, sh_ref, sc_ref, aq_ref, bq_ref, ak_ref, bk_ref, w_ref,
                q_ref, k_ref, vt_ref, h_scr, y_scr):
    j = pl.program_id(1)
    n_q = AH * ADH // TN_QKV
    kv_cols = AKV * ADH

    def project(jj):
        y_scr[jj % 2] = jnp.dot(h_scr[...], w_ref[...], preferred_element_type=F32)

    def finish(jj):
        y_ref = y_scr.at[jj % 2]
        if jj < n_q:
            _norm_rope_heads(y_ref, q_ref, TN_QKV, aq_ref[...], bq_ref[...])
        else:
            _norm_rope_heads(y_ref, k_ref, kv_cols, ak_ref[...], bk_ref[...])
            vt = y_ref[:, kv_cols:].T
            ones = jnp.ones((VT_ONES, vt.shape[1]), BF16)
            for hd in range(AKV):
                vt_ref[hd * VT_ROWS:hd * VT_ROWS + ADH, :] = (
                    vt[hd * ADH:(hd + 1) * ADH, :].astype(BF16))
                vt_ref[hd * VT_ROWS + ADH:(hd + 1) * VT_ROWS, :] = ones

    def step(jj):
        if jj == 0:
            h_scr[...] = _adaln(x_ref[...], g_ref[...], sh_ref[...], sc_ref[...]).astype(BF16)
        project(jj)
        if jj >= 1:
            finish(jj - 1)
        if jj == n_q:
            finish(jj)

    for jj in range(n_q + 1):
        pl.when(j == jj)(functools.partial(step, jj))


def _attn_qkv(x_all, gain, mod, a_q, b_q, a_k, b_k, w_qkv):
    tm, tn = TM_QKV, TN_QKV
    n_q = AH * ADH // tn
    n_lat = N_LAT // tm
    per_seq = SEQ // tm
    kv_cols = AKV * ADH
    assert tn == 2 * kv_cols
    vec = pl.BlockSpec((1, D), lambda i, j: (0, 0))
    rope = pl.BlockSpec((tm, ADH), lambda i, j: (jnp.where(i < n_lat, i % per_seq, per_seq), 0))
    return pl.pallas_call(
        _qkv_kernel,
        out_shape=(jax.ShapeDtypeStruct((N_ALL, AH * ADH), BF16),
                   jax.ShapeDtypeStruct((N_ALL, kv_cols), BF16),
                   jax.ShapeDtypeStruct((AKV * VT_ROWS, N_ALL), BF16)),
        grid=(N_ALL // tm, n_q + 1),
        in_specs=[
            pl.BlockSpec((tm, D), lambda i, j: (i, 0)),
            vec, _mod_spec(tm, 0), _mod_spec(tm, 1), rope, rope, rope, rope,
            pl.BlockSpec((D, tn), lambda i, j: (0, j)),
        ],
        out_specs=(pl.BlockSpec((tm, tn), lambda i, j: (i, jnp.clip(j - 1, 0, n_q - 1))),
                   pl.BlockSpec((tm, kv_cols), lambda i, j: (i, 0)),
                   pl.BlockSpec((AKV * VT_ROWS, tm), lambda i, j: (0, i))),
        scratch_shapes=[pltpu.VMEM((tm, D), BF16), pltpu.VMEM((2, tm, tn), F32)],
        compiler_params=_params(2),
        name="attn_qkv",
    )(x_all, gain, mod, mod, a_q, b_q, a_k, b_k, w_qkv)


def _attn_kernel(q_ref, kl_ref, kc_ref, vtl_ref, vtc_ref, o_ref):
    tq = q_ref.shape[0]
    nq = AG * tq
    q = jnp.concatenate([q_ref[:, g * ADH:(g + 1) * ADH] for g in range(AG)], axis=0)

    def block(k, vt, m, acc):
        st = lax.dot_general(k, q, _NT, preferred_element_type=F32)
        m_new = jnp.maximum(m, jnp.max(st, axis=0, keepdims=True))
        p = jnp.exp2(st - m_new).astype(BF16)
        acc = jnp.exp2(m - m_new) * acc + jnp.dot(vt, p, preferred_element_type=F32)
        return m_new, acc

    m = jnp.full((1, nq), -jnp.inf, F32)
    acc = jnp.zeros((VT_ROWS, nq), F32)
    for c in range(SEQ // TK_ATT):
        m, acc = block(kl_ref[c * TK_ATT:(c + 1) * TK_ATT, :],
                       vtl_ref[:, c * TK_ATT:(c + 1) * TK_ATT], m, acc)
    m, acc = block(kc_ref[...], vtc_ref[...], m, acc)
    out = (acc[:ADH, :] * (1.0 / acc[ADH:ADH + 1, :])).T
    for g in range(AG):
        o_ref[:, g * ADH:(g + 1) * ADH] = out[g * tq:(g + 1) * tq, :].astype(BF16)


def _attention(q, k, vt):
    tq = TQ_ATT
    per_seq = SEQ // tq
    return pl.pallas_call(
        _attn_kernel,
        out_shape=jax.ShapeDtypeStruct((N_LAT, AH * ADH), BF16),
        grid=(BATCH, AKV, per_seq),
        in_specs=[
            pl.BlockSpec((tq, AG * ADH), lambda b, h, i: (b * per_seq + i, h)),
            pl.BlockSpec((SEQ, ADH), lambda b, h, i: (b, h)),
            pl.BlockSpec((CTX, ADH), lambda b, h, i: (N_LAT // CTX + b, h)),
            pl.BlockSpec((VT_ROWS, SEQ), lambda b, h, i: (h, b)),
            pl.BlockSpec((VT_ROWS, CTX), lambda b, h, i: (h, N_LAT // CTX + b)),
        ],
        out_specs=pl.BlockSpec((tq, AG * ADH), lambda b, h, i: (b * per_seq + i, h)),
        compiler_params=_params(3),
        name="attention",
    )(q, k, k, vt, vt)


def _attn_out_kernel(o_ref, x_ref, gp_ref, gt_ref, w_ref, out_ref):
    for r in range(0, TM_ATT_OUT, TM_OUT):
        rows = slice(r, r + TM_OUT)
        y = jnp.dot(o_ref[rows, :], w_ref[...], preferred_element_type=F32)
        out_ref[rows, :] = _post_norm_residual(x_ref[rows, :], y, gp_ref[...], gt_ref[...])


def _attn_out(o, x_all, g_post, mod, w_out):
    tm = TM_ATT_OUT
    return pl.pallas_call(
        _attn_out_kernel,
        out_shape=jax.ShapeDtypeStruct((N_LAT, D), F32),
        grid=(N_LAT // tm,),
        in_specs=[
            pl.BlockSpec((tm, D), lambda i: (i, 0)),
            pl.BlockSpec((tm, D), lambda i: (i, 0)),
            pl.BlockSpec((1, D), lambda i: (0, 0)),
            pl.BlockSpec((None, 1, D), lambda i: (_mod_row(i, tm), 0, 2)),
            pl.BlockSpec((D, D), lambda i: (0, 0)),
        ],
        out_specs=pl.BlockSpec((tm, D), lambda i: (i, 0)),
        compiler_params=_params(1),
        name="attn_out",
    )(o, x_all, g_post, mod, w_out)


def _half_split(a):
    lead = a.shape[:-1]
    n = a.shape[-1] // ADH
    a = a.reshape(*lead, n, ADH // 2, 2)
    return jnp.swapaxes(a, -1, -2).reshape(*lead, n * ADH)


def _qkv_weight_kernel(w_ref, o_ref):
    first_head = pl.program_id(0) * (TN_WPERM // ADH)
    src = lax.broadcasted_iota(jnp.int32, (ADH, ADH), 0)
    dst = lax.broadcasted_iota(jnp.int32, (ADH, ADH), 1)
    half = ADH // 2
    split_src = jnp.where(dst < half, 2 * dst, 2 * (dst - half) + 1)
    for h in range(TN_WPERM // ADH):
        is_qk = first_head + h < AH + AKV
        perm = (src == jnp.where(is_qk, split_src, dst)).astype(BF16)
        cols = slice(h * ADH, (h + 1) * ADH)
        o_ref[:, cols] = jnp.dot(w_ref[:, cols].astype(BF16), perm,
                                 preferred_element_type=F32).astype(BF16)


def _qkv_weight(w_qkv):
    n_cols = w_qkv.shape[1]
    return pl.pallas_call(
        _qkv_weight_kernel,
        out_shape=jax.ShapeDtypeStruct((D, n_cols), BF16),
        grid=(n_cols // TN_WPERM,),
        in_specs=[pl.BlockSpec((D, TN_WPERM), lambda j: (0, j))],
        out_specs=pl.BlockSpec((D, TN_WPERM), lambda j: (0, j)),
        compiler_params=_params(1),
        name="qkv_weight",
    )(w_qkv)


def _rope_tables(gain, scale):
    rows = SEQ // GRID_W
    t_row = jnp.repeat(jnp.arange(rows, dtype=F32), GRID_W)
    t_col = jnp.tile(jnp.arange(GRID_W, dtype=F32), rows)
    per_axis = ADH // 2
    inv = ROPE_THETA ** (-jnp.arange(0, per_axis, 2, dtype=F32) / per_axis)
    ang = jnp.concatenate([t_row[:, None] * inv, t_col[:, None] * inv], axis=-1)
    ang = jnp.concatenate([ang, jnp.zeros((TM_QKV, per_axis), F32)], axis=0)
    cos = jnp.concatenate([jnp.cos(ang), jnp.cos(ang)], axis=-1)
    sin = jnp.concatenate([-jnp.sin(ang), jnp.sin(ang)], axis=-1)
    g = _half_split(gain.astype(F32))
    return (g * scale) * cos, (jnp.roll(g, per_axis) * scale) * sin


def kernel(x, c, ctx, c_ctx, w_mod, b_mod, g_mix_pre, g_mix_post, g_ffn_pre, g_ffn_post,
           w_mlstm_in, b_mlstm_gate, g_mlstm_head, w_mlstm_out,
           w_attn_qkv, g_attn_q, g_attn_k, w_attn_out, w_ffn_in, w_ffn_out):
    assert x.shape == (BATCH, SEQ, D) and ctx.shape == (BATCH, CTX, D)
    xl = x.reshape(N_LAT, D)
    xc = ctx.reshape(N_CTX, D)
    row = lambda a: a.reshape(1, -1)

    c_all = jnp.concatenate([c, c_ctx[None, :], jnp.zeros((MOD_ROWS - BATCH - 1, D), F32)], axis=0)
    mod = _modulation(c_all, w_mod, b_mod).reshape(2, MOD_ROWS, 1, N_MOD * D)

    w_main = w_mlstm_in[0].astype(BF16)
    w_gate = jnp.pad(w_main[:, M_MAIN:], ((0, 0), (0, GATE_PAD - M_GATES)))
    b_gate = jnp.pad(b_mlstm_gate[0], (0, GATE_PAD - M_GATES)).reshape(1, GATE_PAD)
    p, gpre = _mlstm_inproj(xl, xc, row(g_mix_pre[0]), mod[0], w_main, w_gate,
                            row(g_mlstm_head[0]))
    gc, gr = _gateprep(gpre, b_gate)
    hf, hb = _mlstm_scan(p, gc, gr)
    x_all = _mlstm_out(hf, hb, p, xl, xc, row(g_mix_post[0]), mod[0],
                       w_mlstm_out[0].astype(BF16))
    w_ffn_in16 = w_ffn_in.astype(BF16)
    w_ffn_out16 = w_ffn_out.astype(BF16)
    x_all = _ffn(x_all, row(g_ffn_pre[0]), row(g_ffn_post[0]), mod[0], w_ffn_in16, w_ffn_out16, 0)

    w_qkv = _qkv_weight(w_attn_qkv[0])
    a_q, b_q = _rope_tables(g_attn_q[0], A_SCALE * LOG2E)
    a_k, b_k = _rope_tables(g_attn_k[0], 1.0)
    q, k, vt = _attn_qkv(x_all, row(g_mix_pre[1]), mod[1], a_q, b_q, a_k, b_k, w_qkv)
    o = _attention(q, k, vt)
    x_lat = _attn_out(o, x_all, row(g_mix_post[1]), mod[1], w_attn_out[0].astype(BF16))
    x_lat = _ffn(x_lat, row(g_ffn_pre[1]), row(g_ffn_post[1]), mod[1], w_ffn_in16, w_ffn_out16, 1)
    return x_lat.reshape(BATCH, SEQ, D)
```

```python
import functools
import math

import jax
import jax.numpy as jnp
from jax import lax
from jax.experimental import pallas as pl
from jax.experimental.pallas import tpu as pltpu

F32 = jnp.float32
BF16 = jnp.bfloat16

D = 2048
BATCH = 4
SEQ = 4096
CTX = 256
N_LAT = BATCH * SEQ
N_CTX = BATCH * CTX
N_ALL = N_LAT + N_CTX
N_MOD = 6
EPS = 1e-6
MOD_ROWS = 8

MH = 8
MDK = 128
MDV = 256
MQK = MH * MDK
MV = MH * MDV
M_MAIN = 2 * MQK + 2 * MV
M_GATES = 4 * MH
GATE_PAD = 128
SOFTCAP = 15.0
M_SCALE_LOG2 = -0.5 * math.log2(MDK)
LANES = 128

AH = 16
AKV = 4
ADH = 128
AG = AH // AKV
A_SCALE = ADH ** -0.5
LOG2E = math.log2(math.e)
VT_ONES = 16
VT_ROWS = ADH + VT_ONES
GRID_W = 64
ROPE_THETA = 10000.0

FFN = 5632

VMEM_LIMIT = 56 * 1024 * 1024
TM_PROJ = 512
TM_QKV = 1024
TN_INPROJ = 2048
TN_QKV = 1024
TM_OUT = 256
TM_ATT_OUT = 1024
TM_MLSTM_OUT = 2 * TM_OUT
TM_FFN = 1024
TF_FFN = 512
TN_MOD = 1024
TN_WPERM = 512
L_SCAN = 256
GATEPREP_CHUNKS = 4
TQ_ATT = 1024
TK_ATT = 1024

_NT = (((1,), (1,)), ((), ()))


def _params(n_axes):
    return pltpu.CompilerParams(
        dimension_semantics=("arbitrary",) * n_axes, vmem_limit_bytes=VMEM_LIMIT)


def _mod_row(i, tm):
    return jnp.where(i < N_LAT // tm, i // (SEQ // tm), BATCH)


def _mod_spec(tm, chunk):
    return pl.BlockSpec((None, 1, D), lambda i, j: (_mod_row(i, tm), 0, chunk))


def _adaln(x, gain, shift, scale):
    r = lax.rsqrt(jnp.mean(x * x, axis=-1, keepdims=True) + EPS)
    return (x * r) * (gain * (1.0 + scale)) + shift


def _post_norm_residual(x, y, gain, gate):
    r = lax.rsqrt(jnp.mean(y * y, axis=-1, keepdims=True) + EPS)
    return x + (y * r) * (gain * gate)


def _sigmoid(z):
    return 1.0 / (1.0 + jnp.exp2(z * (-LOG2E)))


def _mod_kernel(c_ref, w_ref, b_ref, o_ref):
    c = c_ref[...]
    cond = (c * _sigmoid(c)).astype(BF16)
    o_ref[...] = jnp.dot(cond, w_ref[...].astype(BF16), preferred_element_type=F32) + b_ref[...]


def _modulation(c_all, w_mod, b_mod):
    depth = w_mod.shape[0]
    return pl.pallas_call(
        _mod_kernel,
        out_shape=jax.ShapeDtypeStruct((depth, MOD_ROWS, N_MOD * D), F32),
        grid=(depth, N_MOD * D // TN_MOD),
        in_specs=[
            pl.BlockSpec((MOD_ROWS, D), lambda l, j: (0, 0)),
            pl.BlockSpec((None, D, TN_MOD), lambda l, j: (l, 0, j)),
            pl.BlockSpec((None, 1, TN_MOD), lambda l, j: (l, 0, j)),
        ],
        out_specs=pl.BlockSpec((None, MOD_ROWS, TN_MOD), lambda l, j: (l, 0, j)),
        compiler_params=_params(2),
        name="modulation",
    )(c_all, w_mod, b_mod.reshape(depth, 1, N_MOD * D))


def _inproj_kernel(xl_ref, xc_ref, g_ref, sh_ref, sc_ref, w_ref, wg_ref, gh_ref, p_ref, gpre_ref,
                   h_scr):
    i = pl.program_id(0)
    j = pl.program_id(1)
    o_tile = (2 * MQK + MV) // TN_INPROJ

    @pl.when(j == 0)
    def _():
        is_latent = i < N_LAT // TM_PROJ
        half = TM_PROJ // 2
        for r in range(0, TM_PROJ, half):
            rows = slice(r, r + half)
            x = jnp.where(is_latent, xl_ref[rows, :], xc_ref[rows, :])
            h = _adaln(x, g_ref[...], sh_ref[...], sc_ref[...]).astype(BF16)
            h_scr[rows, :] = h
            gpre_ref[rows, :] = jnp.dot(h, wg_ref[...], preferred_element_type=F32)
            p_ref[rows, :] = jnp.dot(h, w_ref[...], preferred_element_type=F32).astype(BF16)

    @pl.when((j > 0) & (j < o_tile))
    def _():
        p_ref[...] = jnp.dot(h_scr[...], w_ref[...], preferred_element_type=F32).astype(BF16)

    @pl.when(j == o_tile)
    def _():
        o = jnp.dot(h_scr[...], w_ref[...], preferred_element_type=F32)
        p_ref[...] = (_sigmoid(o) * gh_ref[...]).astype(BF16)


def _two_source_specs(tm):
    n_lat = N_LAT // tm
    return [
        pl.BlockSpec((tm, D), lambda i, *_: (jnp.minimum(i, n_lat - 1), 0)),
        pl.BlockSpec((tm, D), lambda i, *_: (jnp.maximum(i - n_lat, 0), 0)),
    ]


def _mlstm_inproj(xl, xc, gain, mod, w_main, w_gate, g_head):
    tm, tn = TM_PROJ, TN_INPROJ
    assert tn == MV and (2 * MQK + MV) % tn == 0
    vec = pl.BlockSpec((1, D), lambda i, j: (0, 0))
    return pl.pallas_call(
        _inproj_kernel,
        out_shape=(jax.ShapeDtypeStruct((N_ALL, M_MAIN), BF16),
                   jax.ShapeDtypeStruct((N_ALL, GATE_PAD), F32)),
        grid=(N_ALL // tm, M_MAIN // tn),
        in_specs=_two_source_specs(tm) + [
            vec, _mod_spec(tm, 0), _mod_spec(tm, 1),
            pl.BlockSpec((D, tn), lambda i, j: (0, j)),
            pl.BlockSpec((D, GATE_PAD), lambda i, j: (0, 0)),
            pl.BlockSpec((1, MV), lambda i, j: (0, 0)),
        ],
        out_specs=(pl.BlockSpec((tm, tn), lambda i, j: (i, j)),
                   pl.BlockSpec((tm, GATE_PAD), lambda i, j: (i, 0))),
        scratch_shapes=[pltpu.VMEM((tm, D), BF16)],
        compiler_params=_params(2),
        name="mlstm_inproj",
    )(xl, xc, gain, mod, mod, w_main, w_gate, g_head)


def _gateprep_kernel(gpre_ref, b_ref, gc_ref, gr_ref):
    for c in range(GATEPREP_CHUNKS):
        rows = slice(c * L_SCAN, (c + 1) * L_SCAN)
        _gateprep_chunk(gpre_ref.at[rows, :], b_ref, gc_ref.at[rows, :], gr_ref.at[:, rows])


def _gateprep_chunk(gpre_ref, b_ref, gc_ref, gr_ref):
    L = gpre_ref.shape[0]
    z = gpre_ref[...] + b_ref[...]
    a = SOFTCAP * jnp.tanh(z * (1.0 / SOFTCAP))
    logsig = jnp.minimum(a, 0.0) - jnp.log(1.0 + jnp.exp(-jnp.abs(a)))
    row = lax.broadcasted_iota(jnp.int32, (L, GATE_PAD), 0)
    lane = lax.broadcasted_iota(jnp.int32, (L, GATE_PAD), 1)

    def scans(x, op, fill):
        pre, suf = x, x
        s = 1
        while s < L:
            pre = op(pre, jnp.where(row >= s, pltpu.roll(pre, s, 0), fill))
            suf = op(suf, jnp.where(row < L - s, pltpu.roll(suf, L - s, 0), fill))
            s *= 2
        return pre, suf

    fwd_lanes = lane < 2 * MH
    b_pre, b_suf = scans(logsig, jnp.add, 0.0)
    b = jnp.where(fwd_lanes, b_pre, b_suf)
    r = a - pltpu.roll(b, GATE_PAD - MH, 1)
    c_pre, c_suf = scans(r, jnp.maximum, -jnp.inf)
    cmax = jnp.where(fwd_lanes, c_pre, c_suf)
    is_gate_lane = (lane < MH) | ((lane >= 2 * MH) & (lane < 3 * MH))
    gc_ref[...] = jnp.where(is_gate_lane, cmax, b) * LOG2E
    gr_ref[...] = (r * LOG2E).T[:M_GATES, :]


def _gateprep(gpre, bias):
    L = GATEPREP_CHUNKS * L_SCAN
    return pl.pallas_call(
        _gateprep_kernel,
        out_shape=(jax.ShapeDtypeStruct((N_ALL, GATE_PAD), F32),
                   jax.ShapeDtypeStruct((M_GATES, N_ALL), F32)),
        grid=(N_ALL // L,),
        in_specs=[pl.BlockSpec((L, GATE_PAD), lambda i: (i, 0)),
                  pl.BlockSpec((1, GATE_PAD), lambda i: (0, 0))],
        out_specs=(pl.BlockSpec((L, GATE_PAD), lambda i: (i, 0)),
                   pl.BlockSpec((M_GATES, L), lambda i: (0, i))),
        compiler_params=_params(1),
        name="mlstm_gateprep",
    )(gpre, bias)


def _scan_unit(q, k, vx, r_row, cmax_col, b_col, mask, end, cx_ref, m_ref, h_out):
    L = q.shape[0]
    m = m_ref[0:1, 0:1]
    u = jnp.maximum(m, cmax_col)
    u_b = jnp.broadcast_to(u - M_SCALE_LOG2, (L, L))
    dmat = jnp.where(mask, jnp.exp2(r_row - u_b), 0.0)
    sm = (lax.dot_general(q, k, _NT, preferred_element_type=F32) * dmat).astype(BF16)
    u_end = u[end:end + 1, :]
    w_row = jnp.exp2(r_row - u_end)
    kw = (k.astype(F32).T * w_row).astype(BF16)
    both = jnp.dot(jnp.concatenate([sm, kw], axis=0), vx, preferred_element_type=F32)
    intra = both[:L, :]
    cx = cx_ref[...]
    inter = jnp.dot(q, cx.astype(BF16), preferred_element_type=F32)
    u_rep = u_b[:, :LANES]
    w_inter = jnp.exp2(m - u_rep)
    den = intra[:, MDV:] + w_inter * inter[:, MDV:]
    floor = jnp.exp2(-(jnp.broadcast_to(b_col, (L, LANES)) + u_rep + M_SCALE_LOG2))
    inv = 1.0 / jnp.maximum(jnp.abs(den), floor)
    for t in range(MDV // LANES):
        cols = slice(t * LANES, (t + 1) * LANES)
        h_out[:, cols] = ((intra[:, cols] + w_inter * inter[:, cols]) * inv).astype(h_out.dtype)

    cx_ref[...] = jnp.exp2(m - u_end) * cx + both[L:, :]
    m_ref[...] = jnp.broadcast_to(b_col[end:end + 1, :] + u_end, m_ref.shape)


def _scan_kernel(qf, kf, vf, gcf, grf, qb, kb, vb, gcb, grb, hf_ref, hb_ref, cx_scr, m_scr):
    L = qf.shape[0]

    @pl.when(pl.program_id(1) == 0)
    def _():
        cx_scr[...] = jnp.zeros_like(cx_scr)
        m_scr[...] = jnp.zeros_like(m_scr)

    row = lax.broadcasted_iota(jnp.int32, (L, L), 0)
    col = lax.broadcasted_iota(jnp.int32, (L, L), 1)
    ones = jnp.ones((L, LANES), BF16)
    directions = (
        (qf, kf, vf, gcf, grf, hf_ref, col <= row, 0, L - 1),
        (qb, kb, vb, gcb, grb, hb_ref, col >= row, 2 * MH, 0),
    )
    for d, (q_ref, k_ref, v_ref, gc_ref, gr_ref, h_ref, mask, goff, end) in enumerate(directions):
        for h in range(MH):
            ig, fg = goff + h, goff + MH + h
            _scan_unit(
                q_ref[:, h * MDK:(h + 1) * MDK], k_ref[:, h * MDK:(h + 1) * MDK],
                jnp.concatenate([v_ref[:, h * MDV:(h + 1) * MDV], ones], axis=1),
                gr_ref[ig:ig + 1, :], gc_ref[:, ig:ig + 1], gc_ref[:, fg:fg + 1],
                mask, end, cx_scr.at[d * MH + h], m_scr.at[d * MH + h],
                h_ref.at[:, h * MDV:(h + 1) * MDV])


def _mlstm_scan(p, gc, gr):
    L = L_SCAN
    n_ctx_chunks = CTX // L
    n_lat_chunks = SEQ // L
    steps = n_ctx_chunks + n_lat_chunks
    ctx0 = N_LAT // L

    def fwd(b, s):
        return jnp.where(s < n_ctx_chunks, ctx0 + b * n_ctx_chunks + s,
                         b * n_lat_chunks + s - n_ctx_chunks)

    def bwd(b, s):
        return jnp.where(s < n_ctx_chunks, ctx0 + b * n_ctx_chunks + (n_ctx_chunks - 1 - s),
                         b * n_lat_chunks + (n_lat_chunks - 1 - (s - n_ctx_chunks)))

    def specs(idx):
        return [
            pl.BlockSpec((L, MQK), lambda b, s: (idx(b, s), 0)),
            pl.BlockSpec((L, MQK), lambda b, s: (idx(b, s), 1)),
            pl.BlockSpec((L, MV), lambda b, s: (idx(b, s), 1)),
            pl.BlockSpec((L, GATE_PAD), lambda b, s: (idx(b, s), 0)),
            pl.BlockSpec((M_GATES, L), lambda b, s: (0, idx(b, s))),
        ]

    return pl.pallas_call(
        _scan_kernel,
        out_shape=(jax.ShapeDtypeStruct((N_ALL, MV), BF16),
                   jax.ShapeDtypeStruct((N_ALL, MV), BF16)),
        grid=(BATCH, steps),
        in_specs=specs(fwd) + specs(bwd),
        out_specs=(pl.BlockSpec((L, MV), lambda b, s: (fwd(b, s), 0)),
                   pl.BlockSpec((L, MV), lambda b, s: (bwd(b, s), 0))),
        scratch_shapes=[pltpu.VMEM((2 * MH, MDK, MDV + LANES), F32),
                        pltpu.VMEM((2 * MH, 8, LANES), F32)],
        compiler_params=_params(2),
        name="mlstm_scan",
    )(p, p, p, gc, gr, p, p, p, gc, gr)


def _mlstm_out_kernel(hf_ref, hb_ref, og_ref, xl_ref, xc_ref, gp_ref, gt_ref, w_ref,
                      out_ref, a_scr):
    is_latent = pl.program_id(0) < N_LAT // TM_MLSTM_OUT
    for r0 in range(0, TM_MLSTM_OUT, TM_OUT):
        rows = slice(r0, r0 + TM_OUT)
        for h in range(MH):
            sl = slice(h * MDV, (h + 1) * MDV)
            hh = hf_ref[rows, sl].astype(F32) + hb_ref[rows, sl].astype(F32)
            r = lax.rsqrt(jnp.mean(hh * hh, axis=-1, keepdims=True) + EPS)
            a_scr[rows, sl] = ((hh * r) * og_ref[rows, sl].astype(F32)).astype(BF16)
        y = jnp.dot(a_scr[rows, :], w_ref[...], preferred_element_type=F32)
        x = jnp.where(is_latent, xl_ref[rows, :], xc_ref[rows, :])
        out_ref[rows, :] = _post_norm_residual(x, y, gp_ref[...], gt_ref[...])


def _mlstm_out(hf, hb, p, xl, xc, g_post, mod, w_out):
    tm = TM_MLSTM_OUT
    vec = pl.BlockSpec((1, D), lambda i: (0, 0))
    return pl.pallas_call(
        _mlstm_out_kernel,
        out_shape=jax.ShapeDtypeStruct((N_ALL, D), F32),
        grid=(N_ALL // tm,),
        in_specs=[
            pl.BlockSpec((tm, MV), lambda i: (i, 0)),
            pl.BlockSpec((tm, MV), lambda i: (i, 0)),
            pl.BlockSpec((tm, MV), lambda i: (i, 2)),
        ] + _two_source_specs(tm) + [
            vec,
            pl.BlockSpec((None, 1, D), lambda i: (_mod_row(i, tm), 0, 2)),
            pl.BlockSpec((MV, D), lambda i: (0, 0), pipeline_mode=pl.Buffered(1)),
        ],
        out_specs=pl.BlockSpec((tm, D), lambda i: (i, 0)),
        scratch_shapes=[pltpu.VMEM((tm, MV), BF16)],
        compiler_params=_params(1),
        name="mlstm_out",
    )(hf, hb, p, xl, xc, g_post, mod, w_out)


def _ffn_kernel(x_ref, g_ref, sh_ref, sc_ref, gt_ref, gp_ref, wg_ref, wu_ref, wo_ref, out_ref,
                h_scr):
    f = pl.program_id(1)
    last_f = pl.num_programs(1) - 1
    half = TM_FFN // 2

    def step(first, last):
        for r in range(0, TM_FFN, half):
            rows = slice(r, r + half)
            if first:
                h_scr[rows, :] = _adaln(x_ref[rows, :], g_ref[...], sh_ref[...],
                                        sc_ref[...]).astype(BF16)
            h = h_scr[rows, :]
            gate = jnp.dot(h, wg_ref[...], preferred_element_type=F32)
            up = jnp.dot(h, wu_ref[...], preferred_element_type=F32)
            act = ((gate * _sigmoid(gate)) * up).astype(BF16)
            y = jnp.dot(act, wo_ref[...], preferred_element_type=F32)
            if not first:
                y = out_ref[rows, :] + y
            if last:
                y = _post_norm_residual(x_ref[rows, :], y, gp_ref[...], gt_ref[...])
            out_ref[rows, :] = y

    pl.when(f == 0)(functools.partial(step, True, False))
    pl.when((f > 0) & (f < last_f))(functools.partial(step, False, False))
    pl.when(f == last_f)(functools.partial(step, False, True))


def _ffn(x_rows, g_pre, g_post, mod, w_in, w_out, layer):
    n_rows = x_rows.shape[0]
    tm, tf = TM_FFN, TF_FFN
    nf = FFN // tf
    assert nf >= 2
    vec = pl.BlockSpec((1, D), lambda i, f: (0, 0))
    return pl.pallas_call(
        _ffn_kernel,
        out_shape=jax.ShapeDtypeStruct((n_rows, D), F32),
        grid=(n_rows // tm, nf),
        in_specs=[
            pl.BlockSpec((tm, D), lambda i, f: (i, 0)),
            vec, _mod_spec(tm, 3), _mod_spec(tm, 4), _mod_spec(tm, 5), vec,
            pl.BlockSpec((None, D, tf), lambda i, f: (layer, 0, f)),
            pl.BlockSpec((None, D, tf), lambda i, f: (layer, 0, nf + f)),
            pl.BlockSpec((None, tf, D), lambda i, f: (layer, f, 0)),
        ],
        out_specs=pl.BlockSpec((tm, D), lambda i, f: (i, 0)),
        scratch_shapes=[pltpu.VMEM((tm, D), BF16)],
        compiler_params=_params(2),
        name="ffn",
    )(x_rows, g_pre, mod, mod, mod, g_post, w_in, w_in, w_out)


def _norm_rope_heads(y_ref, out_ref, n_cols, a, b):
    for c in range(0, n_cols, ADH):
        yy = y_ref[:, c:c + ADH]
        r = lax.rsqrt(jnp.mean(yy * yy, axis=-1, keepdims=True) + EPS)
        out_ref[:, c:c + ADH] = ((yy * a + pltpu.roll(yy, ADH // 2, 1) * b) * r).astype(BF16)


def _qkv_kernel(x_ref, g_ref, sh_ref, sc_ref, aq_ref, bq_ref, ak_ref, bk_ref, w_ref,
                q_ref, k_ref, vt_ref, h_scr, y_scr):
    j = pl.program_id(1)
    n_q = AH * ADH // TN_QKV
    kv_cols = AKV * ADH

    def project(jj):
        y_scr[jj % 2] = jnp.dot(h_scr[...], w_ref[...], preferred_element_type=F32)

    def finish(jj):
        y_ref = y_scr.at[jj % 2]
        if jj < n_q:
            _norm_rope_heads(y_ref, q_ref, TN_QKV, aq_ref[...], bq_ref[...])
        else:
            _norm_rope_heads(y_ref, k_ref, kv_cols, ak_ref[...], bk_ref[...])
            vt = y_ref[:, kv_cols:].T
            ones = jnp.ones((VT_ONES, vt.shape[1]), BF16)
            for hd in range(AKV):
                vt_ref[hd * VT_ROWS:hd * VT_ROWS + ADH, :] = (
                    vt[hd * ADH:(hd + 1) * ADH, :].astype(BF16))
                vt_ref[hd * VT_ROWS + ADH:(hd + 1) * VT_ROWS, :] = ones

    def step(jj):
        if jj == 0:
            h_scr[...] = _adaln(x_ref[...], g_ref[...], sh_ref[...], sc_ref[...]).astype(BF16)
        project(jj)
        if jj >= 1:
            finish(jj - 1)
        if jj == n_q:
            finish(jj)

    for jj in range(n_q + 1):
        pl.when(j == jj)(functools.partial(step, jj))


def _attn_qkv(x_all, gain, mod, a_q, b_q, a_k, b_k, w_qkv):
    tm, tn = TM_QKV, TN_QKV
    n_q = AH * ADH // tn
    n_lat = N_LAT // tm
    per_seq = SEQ // tm
    kv_cols = AKV * ADH
    assert tn == 2 * kv_cols
    vec = pl.BlockSpec((1, D), lambda i, j: (0, 0))
    rope = pl.BlockSpec((tm, ADH), lambda i, j: (jnp.where(i < n_lat, i % per_seq, per_seq), 0))
    return pl.pallas_call(
        _qkv_kernel,
        out_shape=(jax.ShapeDtypeStruct((N_ALL, AH * ADH), BF16),
                   jax.ShapeDtypeStruct((N_ALL, kv_cols), BF16),
                   jax.ShapeDtypeStruct((AKV * VT_ROWS, N_ALL), BF16)),
        grid=(N_ALL // tm, n_q + 1),
        in_specs=[
            pl.BlockSpec((tm, D), lambda i, j: (i, 0)),
            vec, _mod_spec(tm, 0), _mod_spec(tm, 1), rope, rope, rope, rope,
            pl.BlockSpec((D, tn), lambda i, j: (0, j)),
        ],
        out_specs=(pl.BlockSpec((tm, tn), lambda i, j: (i, jnp.clip(j - 1, 0, n_q - 1))),
                   pl.BlockSpec((tm, kv_cols), lambda i, j: (i, 0)),
                   pl.BlockSpec((AKV * VT_ROWS, tm), lambda i, j: (0, i))),
        scratch_shapes=[pltpu.VMEM((tm, D), BF16), pltpu.VMEM((2, tm, tn), F32)],
        compiler_params=_params(2),
        name="attn_qkv",
    )(x_all, gain, mod, mod, a_q, b_q, a_k, b_k, w_qkv)


def _attn_kernel(q_ref, kl_ref, kc_ref, vtl_ref, vtc_ref, o_ref):
    tq = q_ref.shape[0]
    nq = AG * tq
    q = jnp.concatenate([q_ref[:, g * ADH:(g + 1) * ADH] for g in range(AG)], axis=0)

    def block(k, vt, m, acc):
        st = lax.dot_general(k, q, _NT, preferred_element_type=F32)
        m_new = jnp.maximum(m, jnp.max(st, axis=0, keepdims=True))
        p = jnp.exp2(st - m_new).astype(BF16)
        acc = jnp.exp2(m - m_new) * acc + jnp.dot(vt, p, preferred_element_type=F32)
        return m_new, acc

    m = jnp.full((1, nq), -jnp.inf, F32)
    acc = jnp.zeros((VT_ROWS, nq), F32)
    for c in range(SEQ // TK_ATT):
        m, acc = block(kl_ref[c * TK_ATT:(c + 1) * TK_ATT, :],
                       vtl_ref[:, c * TK_ATT:(c + 1) * TK_ATT], m, acc)
    m, acc = block(kc_ref[...], vtc_ref[...], m, acc)
    out = (acc[:ADH, :] * (1.0 / acc[ADH:ADH + 1, :])).T
    for g in range(AG):
        o_ref[:, g * ADH:(g + 1) * ADH] = out[g * tq:(g + 1) * tq, :].astype(BF16)


def _attention(q, k, vt):
    tq = TQ_ATT
    per_seq = SEQ // tq
    return pl.pallas_call(
        _attn_kernel,
        out_shape=jax.ShapeDtypeStruct((N_LAT, AH * ADH), BF16),
        grid=(BATCH, AKV, per_seq),
        in_specs=[
            pl.BlockSpec((tq, AG * ADH), lambda b, h, i: (b * per_seq + i, h)),
            pl.BlockSpec((SEQ, ADH), lambda b, h, i: (b, h)),
            pl.BlockSpec((CTX, ADH), lambda b, h, i: (N_LAT // CTX + b, h)),
            pl.BlockSpec((VT_ROWS, SEQ), lambda b, h, i: (h, b)),
            pl.BlockSpec((VT_ROWS, CTX), lambda b, h, i: (h, N_LAT // CTX + b)),
        ],
        out_specs=pl.BlockSpec((tq, AG * ADH), lambda b, h, i: (b * per_seq + i, h)),
        compiler_params=_params(3),
        name="attention",
    )(q, k, k, vt, vt)


def _attn_out_kernel(o_ref, x_ref, gp_ref, gt_ref, w_ref, out_ref):
    for r in range(0, TM_ATT_OUT, TM_ATT_OUT // 2):
        rows = slice(r, r + TM_ATT_OUT // 2)
        y = jnp.dot(o_ref[rows, :], w_ref[...], preferred_element_type=F32)
        out_ref[rows, :] = _post_norm_residual(x_ref[rows, :], y, gp_ref[...], gt_ref[...])


def _attn_out(o, x_all, g_post, mod, w_out):
    tm = TM_ATT_OUT
    return pl.pallas_call(
        _attn_out_kernel,
        out_shape=jax.ShapeDtypeStruct((N_LAT, D), F32),
        grid=(N_LAT // tm,),
        in_specs=[
            pl.BlockSpec((tm, D), lambda i: (i, 0)),
            pl.BlockSpec((tm, D), lambda i: (i, 0)),
            pl.BlockSpec((1, D), lambda i: (0, 0)),
            pl.BlockSpec((None, 1, D), lambda i: (_mod_row(i, tm), 0, 2)),
            pl.BlockSpec((D, D), lambda i: (0, 0), pipeline_mode=pl.Buffered(1)),
        ],
        out_specs=pl.BlockSpec((tm, D), lambda i: (i, 0)),
        compiler_params=_params(1),
        name="attn_out",
    )(o, x_all, g_post, mod, w_out)


def _half_split(a):
    lead = a.shape[:-1]
    n = a.shape[-1] // ADH
    a = a.reshape(*lead, n, ADH // 2, 2)
    return jnp.swapaxes(a, -1, -2).reshape(*lead, n * ADH)


def _qkv_weight_kernel(w_ref, o_ref):
    first_head = pl.program_id(0) * (TN_WPERM // ADH)
    src = lax.broadcasted_iota(jnp.int32, (ADH, ADH), 0)
    dst = lax.broadcasted_iota(jnp.int32, (ADH, ADH), 1)
    half = ADH // 2
    split_src = jnp.where(dst < half, 2 * dst, 2 * (dst - half) + 1)
    for h in range(TN_WPERM // ADH):
        is_qk = first_head + h < AH + AKV
        perm = (src == jnp.where(is_qk, split_src, dst)).astype(BF16)
        cols = slice(h * ADH, (h + 1) * ADH)
        o_ref[:, cols] = jnp.dot(w_ref[:, cols].astype(BF16), perm,
                                 preferred_element_type=F32).astype(BF16)


def _qkv_weight(w_qkv):
    n_cols = w_qkv.shape[1]
    return pl.pallas_call(
        _qkv_weight_kernel,
        out_shape=jax.ShapeDtypeStruct((D, n_cols), BF16),
        grid=(n_cols // TN_WPERM,),
        in_specs=[pl.BlockSpec((D, TN_WPERM), lambda j: (0, j))],
        out_specs=pl.BlockSpec((D, TN_WPERM), lambda j: (0, j)),
        compiler_params=_params(1),
        name="qkv_weight",
    )(w_qkv)


def _rope_tables(gain, scale):
    rows = SEQ // GRID_W
    t_row = jnp.repeat(jnp.arange(rows, dtype=F32), GRID_W)
    t_col = jnp.tile(jnp.arange(GRID_W, dtype=F32), rows)
    per_axis = ADH // 2
    inv = ROPE_THETA ** (-jnp.arange(0, per_axis, 2, dtype=F32) / per_axis)
    ang = jnp.concatenate([t_row[:, None] * inv, t_col[:, None] * inv], axis=-1)
    ang = jnp.concatenate([ang, jnp.zeros((TM_QKV, per_axis), F32)], axis=0)
    cos = jnp.concatenate([jnp.cos(ang), jnp.cos(ang)], axis=-1)
    sin = jnp.concatenate([-jnp.sin(ang), jnp.sin(ang)], axis=-1)
    g = _half_split(gain.astype(F32))
    return (g * scale) * cos, (jnp.roll(g, per_axis) * scale) * sin


def kernel(x, c, ctx, c_ctx, w_mod, b_mod, g_mix_pre, g_mix_post, g_ffn_pre, g_ffn_post,
           w_mlstm_in, b_mlstm_gate, g_mlstm_head, w_mlstm_out,
           w_attn_qkv, g_attn_q, g_attn_k, w_attn_out, w_ffn_in, w_ffn_out):
    assert x.shape == (BATCH, SEQ, D) and ctx.shape == (BATCH, CTX, D)
    xl = x.reshape(N_LAT, D)
    xc = ctx.reshape(N_CTX, D)
    row = lambda a: a.reshape(1, -1)

    c_all = jnp.concatenate([c, c_ctx[None, :], jnp.zeros((MOD_ROWS - BATCH - 1, D), F32)], axis=0)
    mod = _modulation(c_all, w_mod, b_mod).reshape(2, MOD_ROWS, 1, N_MOD * D)

    w_main = w_mlstm_in[0].astype(BF16)
    w_gate = jnp.pad(w_main[:, M_MAIN:], ((0, 0), (0, GATE_PAD - M_GATES)))
    b_gate = jnp.pad(b_mlstm_gate[0], (0, GATE_PAD - M_GATES)).reshape(1, GATE_PAD)
    p, gpre = _mlstm_inproj(xl, xc, row(g_mix_pre[0]), mod[0], w_main, w_gate,
                            row(g_mlstm_head[0]))
    gc, gr = _gateprep(gpre, b_gate)
    hf, hb = _mlstm_scan(p, gc, gr)
    x_all = _mlstm_out(hf, hb, p, xl, xc, row(g_mix_post[0]), mod[0],
                       w_mlstm_out[0].astype(BF16))
    w_ffn_in16 = w_ffn_in.astype(BF16)
    w_ffn_out16 = w_ffn_out.astype(BF16)
    x_all = _ffn(x_all, row(g_ffn_pre[0]), row(g_ffn_post[0]), mod[0], w_ffn_in16, w_ffn_out16, 0)

    w_qkv = _qkv_weight(w_attn_qkv[0])
    a_q, b_q = _rope_tables(g_attn_q[0], A_SCALE * LOG2E)
    a_k, b_k = _rope_tables(g_attn_k[0], 1.0)
    q, k, vt = _attn_qkv(x_all, row(g_mix_pre[1]), mod[1], a_q, b_q, a_k, b_k, w_qkv)
    o = _attention(q, k, vt)
    x_lat = _attn_out(o, x_all, row(g_mix_post[1]), mod[1], w_attn_out[0].astype(BF16))
    x_lat = _ffn(x_lat, row(g_ffn_pre[1]), row(g_ffn_post[1]), mod[1], w_ffn_in16, w_ffn_out16, 1)
    return x_lat.reshape(BATCH, SEQ, D)
```

```python
import functools
import math

import jax
import jax.numpy as jnp
from jax import lax
from jax.experimental import pallas as pl
from jax.experimental.pallas import tpu as pltpu

F32 = jnp.float32
BF16 = jnp.bfloat16

D = 2048
BATCH = 4
SEQ = 4096
CTX = 256
N_LAT = BATCH * SEQ
N_CTX = BATCH * CTX
N_ALL = N_LAT + N_CTX
N_MOD = 6
EPS = 1e-6
MOD_ROWS = 8

MH = 8
MDK = 128
MDV = 256
MQK = MH * MDK
MV = MH * MDV
M_MAIN = 2 * MQK + 2 * MV
M_GATES = 4 * MH
GATE_PAD = 128
SOFTCAP = 15.0
M_SCALE_LOG2 = -0.5 * math.log2(MDK)
LANES = 128

AH = 16
AKV = 4
ADH = 128
AG = AH // AKV
A_SCALE = ADH ** -0.5
LOG2E = math.log2(math.e)
VT_ONES = 16
VT_ROWS = ADH + VT_ONES
GRID_W = 64
ROPE_THETA = 10000.0

FFN = 5632

VMEM_LIMIT = 56 * 1024 * 1024
TM_PROJ = 512
TM_QKV = 1024
TN_INPROJ = 2048
TN_QKV = 1024
TM_OUT = 256
TM_ATT_OUT = 1024
TM_MLSTM_OUT = 2 * TM_OUT
TM_FFN = 1024
TF_FFN = 512
TN_MOD = 1024
TN_WPERM = 512
L_SCAN = 256
GATEPREP_CHUNKS = 4
TQ_ATT = 1024
TK_ATT = 1024

_NT = (((1,), (1,)), ((), ()))


def _params(n_axes):
    return pltpu.CompilerParams(
        dimension_semantics=("arbitrary",) * n_axes, vmem_limit_bytes=VMEM_LIMIT)


def _mod_row(i, tm):
    return jnp.where(i < N_LAT // tm, i // (SEQ // tm), BATCH)


def _mod_spec(tm, chunk):
    return pl.BlockSpec((None, 1, D), lambda i, j: (_mod_row(i, tm), 0, chunk))


def _adaln(x, gain, shift, scale):
    r = lax.rsqrt(jnp.mean(x * x, axis=-1, keepdims=True) + EPS)
    return (x * r) * (gain * (1.0 + scale)) + shift


def _post_norm_residual(x, y, gain, gate):
    r = lax.rsqrt(jnp.mean(y * y, axis=-1, keepdims=True) + EPS)
    return x + (y * r) * (gain * gate)


def _sigmoid(z):
    return 1.0 / (1.0 + jnp.exp2(z * (-LOG2E)))


def _mod_kernel(c_ref, w_ref, b_ref, o_ref):
    c = c_ref[...]
    cond = (c * _sigmoid(c)).astype(BF16)
    o_ref[...] = jnp.dot(cond, w_ref[...].astype(BF16), preferred_element_type=F32) + b_ref[...]


def _modulation(c_all, w_mod, b_mod):
    depth = w_mod.shape[0]
    return pl.pallas_call(
        _mod_kernel,
        out_shape=jax.ShapeDtypeStruct((depth, MOD_ROWS, N_MOD * D), F32),
        grid=(depth, N_MOD * D // TN_MOD),
        in_specs=[
            pl.BlockSpec((MOD_ROWS, D), lambda l, j: (0, 0)),
            pl.BlockSpec((None, D, TN_MOD), lambda l, j: (l, 0, j)),
            pl.BlockSpec((None, 1, TN_MOD), lambda l, j: (l, 0, j)),
        ],
        out_specs=pl.BlockSpec((None, MOD_ROWS, TN_MOD), lambda l, j: (l, 0, j)),
        compiler_params=_params(2),
        name="modulation",
    )(c_all, w_mod, b_mod.reshape(depth, 1, N_MOD * D))


def _inproj_kernel(xl_ref, xc_ref, g_ref, sh_ref, sc_ref, w_ref, wg_ref, gh_ref, p_ref, gpre_ref,
                   xall_ref, h_scr):
    i = pl.program_id(0)
    j = pl.program_id(1)
    o_tile = (2 * MQK + MV) // TN_INPROJ

    @pl.when(j == 0)
    def _():
        is_latent = i < N_LAT // TM_PROJ
        half = TM_PROJ // 2
        for r in range(0, TM_PROJ, half):
            rows = slice(r, r + half)
            x = jnp.where(is_latent, xl_ref[rows, :], xc_ref[rows, :])
            xall_ref[rows, :] = x
            h = _adaln(x, g_ref[...], sh_ref[...], sc_ref[...]).astype(BF16)
            h_scr[rows, :] = h
            gpre_ref[rows, :] = jnp.dot(h, wg_ref[...], preferred_element_type=F32)
            p_ref[rows, :] = jnp.dot(h, w_ref[...], preferred_element_type=F32).astype(BF16)

    @pl.when((j > 0) & (j < o_tile))
    def _():
        p_ref[...] = jnp.dot(h_scr[...], w_ref[...], preferred_element_type=F32).astype(BF16)

    @pl.when(j == o_tile)
    def _():
        o = jnp.dot(h_scr[...], w_ref[...], preferred_element_type=F32)
        p_ref[...] = (_sigmoid(o) * gh_ref[...]).astype(BF16)


def _two_source_specs(tm):
    n_lat = N_LAT // tm
    return [
        pl.BlockSpec((tm, D), lambda i, *_: (jnp.minimum(i, n_lat - 1), 0)),
        pl.BlockSpec((tm, D), lambda i, *_: (jnp.maximum(i - n_lat, 0), 0)),
    ]


def _mlstm_inproj(xl, xc, gain, mod, w_main, w_gate, g_head):
    tm, tn = TM_PROJ, TN_INPROJ
    assert tn == MV and (2 * MQK + MV) % tn == 0
    vec = pl.BlockSpec((1, D), lambda i, j: (0, 0))
    return pl.pallas_call(
        _inproj_kernel,
        out_shape=(jax.ShapeDtypeStruct((N_ALL, M_MAIN), BF16),
                   jax.ShapeDtypeStruct((N_ALL, GATE_PAD), F32),
                   jax.ShapeDtypeStruct((N_ALL, D), F32)),
        grid=(N_ALL // tm, M_MAIN // tn),
        in_specs=_two_source_specs(tm) + [
            vec, _mod_spec(tm, 0), _mod_spec(tm, 1),
            pl.BlockSpec((D, tn), lambda i, j: (0, j)),
            pl.BlockSpec((D, GATE_PAD), lambda i, j: (0, 0)),
            pl.BlockSpec((1, MV), lambda i, j: (0, 0)),
        ],
        out_specs=(pl.BlockSpec((tm, tn), lambda i, j: (i, j)),
                   pl.BlockSpec((tm, GATE_PAD), lambda i, j: (i, 0)),
                   pl.BlockSpec((tm, D), lambda i, j: (i, 0))),
        scratch_shapes=[pltpu.VMEM((tm, D), BF16)],
        compiler_params=_params(2),
        name="mlstm_inproj",
    )(xl, xc, gain, mod, mod, w_main, w_gate, g_head)


def _gateprep_kernel(gpre_ref, b_ref, gc_ref, gr_ref):
    for c in range(GATEPREP_CHUNKS):
        rows = slice(c * L_SCAN, (c + 1) * L_SCAN)
        _gateprep_chunk(gpre_ref.at[rows, :], b_ref, gc_ref.at[rows, :], gr_ref.at[:, rows])


def _gateprep_chunk(gpre_ref, b_ref, gc_ref, gr_ref):
    L = gpre_ref.shape[0]
    z = gpre_ref[...] + b_ref[...]
    a = SOFTCAP * jnp.tanh(z * (1.0 / SOFTCAP))
    logsig = jnp.minimum(a, 0.0) - jnp.log(1.0 + jnp.exp(-jnp.abs(a)))
    row = lax.broadcasted_iota(jnp.int32, (L, GATE_PAD), 0)
    lane = lax.broadcasted_iota(jnp.int32, (L, GATE_PAD), 1)

    def scans(x, op, fill):
        pre, suf = x, x
        s = 1
        while s < L:
            pre = op(pre, jnp.where(row >= s, pltpu.roll(pre, s, 0), fill))
            suf = op(suf, jnp.where(row < L - s, pltpu.roll(suf, L - s, 0), fill))
            s *= 2
        return pre, suf

    fwd_lanes = lane < 2 * MH
    b_pre, b_suf = scans(logsig, jnp.add, 0.0)
    b = jnp.where(fwd_lanes, b_pre, b_suf)
    r = a - pltpu.roll(b, GATE_PAD - MH, 1)
    c_pre, c_suf = scans(r, jnp.maximum, -jnp.inf)
    cmax = jnp.where(fwd_lanes, c_pre, c_suf)
    is_gate_lane = (lane < MH) | ((lane >= 2 * MH) & (lane < 3 * MH))
    gc_ref[...] = jnp.where(is_gate_lane, cmax, b) * LOG2E
    gr_ref[...] = (r * LOG2E).T[:M_GATES, :]


def _gateprep(gpre, bias):
    L = GATEPREP_CHUNKS * L_SCAN
    return pl.pallas_call(
        _gateprep_kernel,
        out_shape=(jax.ShapeDtypeStruct((N_ALL, GATE_PAD), F32),
                   jax.ShapeDtypeStruct((M_GATES, N_ALL), F32)),
        grid=(N_ALL // L,),
        in_specs=[pl.BlockSpec((L, GATE_PAD), lambda i: (i, 0)),
                  pl.BlockSpec((1, GATE_PAD), lambda i: (0, 0))],
        out_specs=(pl.BlockSpec((L, GATE_PAD), lambda i: (i, 0)),
                   pl.BlockSpec((M_GATES, L), lambda i: (0, i))),
        compiler_params=_params(1),
        name="mlstm_gateprep",
    )(gpre, bias)


def _scan_unit(q, k, vx, r_row, cmax_col, b_col, mask, end, cx_ref, m_ref, h_out):
    L = q.shape[0]
    m = m_ref[0:1, 0:1]
    u = jnp.maximum(m, cmax_col)
    u_b = jnp.broadcast_to(u - M_SCALE_LOG2, (L, L))
    dmat = jnp.where(mask, jnp.exp2(r_row - u_b), 0.0)
    sm = (lax.dot_general(q, k, _NT, preferred_element_type=F32) * dmat).astype(BF16)
    u_end = u[end:end + 1, :]
    w_row = jnp.exp2(r_row - u_end)
    kw = (k.astype(F32).T * w_row).astype(BF16)
    both = jnp.dot(jnp.concatenate([sm, kw], axis=0), vx, preferred_element_type=F32)
    intra = both[:L, :]
    cx = cx_ref[...]
    inter = jnp.dot(q, cx.astype(BF16), preferred_element_type=F32)
    u_rep = u_b[:, :LANES]
    w_inter = jnp.exp2(m - u_rep)
    den = intra[:, MDV:] + w_inter * inter[:, MDV:]
    floor = jnp.exp2(-(jnp.broadcast_to(b_col, (L, LANES)) + u_rep + M_SCALE_LOG2))
    inv = 1.0 / jnp.maximum(jnp.abs(den), floor)
    for t in range(MDV // LANES):
        cols = slice(t * LANES, (t + 1) * LANES)
        h_out[:, cols] = ((intra[:, cols] + w_inter * inter[:, cols]) * inv).astype(h_out.dtype)

    cx_ref[...] = jnp.exp2(m - u_end) * cx + both[L:, :]
    m_ref[...] = jnp.broadcast_to(b_col[end:end + 1, :] + u_end, m_ref.shape)


def _scan_kernel(qf, kf, vf, gcf, grf, qb, kb, vb, gcb, grb, hf_ref, hb_ref, cx_scr, m_scr):
    L = qf.shape[0]

    @pl.when(pl.program_id(1) == 0)
    def _():
        cx_scr[...] = jnp.zeros_like(cx_scr)
        m_scr[...] = jnp.zeros_like(m_scr)

    row = lax.broadcasted_iota(jnp.int32, (L, L), 0)
    col = lax.broadcasted_iota(jnp.int32, (L, L), 1)
    ones = jnp.ones((L, LANES), BF16)
    directions = (
        (qf, kf, vf, gcf, grf, hf_ref, col <= row, 0, L - 1),
        (qb, kb, vb, gcb, grb, hb_ref, col >= row, 2 * MH, 0),
    )
    for d, (q_ref, k_ref, v_ref, gc_ref, gr_ref, h_ref, mask, goff, end) in enumerate(directions):
        for h in range(MH):
            ig, fg = goff + h, goff + MH + h
            _scan_unit(
                q_ref[:, h * MDK:(h + 1) * MDK], k_ref[:, h * MDK:(h + 1) * MDK],
                jnp.concatenate([v_ref[:, h * MDV:(h + 1) * MDV], ones], axis=1),
                gr_ref[ig:ig + 1, :], gc_ref[:, ig:ig + 1], gc_ref[:, fg:fg + 1],
                mask, end, cx_scr.at[d * MH + h], m_scr.at[d * MH + h],
                h_ref.at[:, h * MDV:(h + 1) * MDV])


def _mlstm_scan(p, gc, gr):
    L = L_SCAN
    n_ctx_chunks = CTX // L
    n_lat_chunks = SEQ // L
    steps = n_ctx_chunks + n_lat_chunks
    ctx0 = N_LAT // L

    def fwd(b, s):
        return jnp.where(s < n_ctx_chunks, ctx0 + b * n_ctx_chunks + s,
                         b * n_lat_chunks + s - n_ctx_chunks)

    def bwd(b, s):
        return jnp.where(s < n_ctx_chunks, ctx0 + b * n_ctx_chunks + (n_ctx_chunks - 1 - s),
                         b * n_lat_chunks + (n_lat_chunks - 1 - (s - n_ctx_chunks)))

    def specs(idx):
        return [
            pl.BlockSpec((L, MQK), lambda b, s: (idx(b, s), 0)),
            pl.BlockSpec((L, MQK), lambda b, s: (idx(b, s), 1)),
            pl.BlockSpec((L, MV), lambda b, s: (idx(b, s), 1)),
            pl.BlockSpec((L, GATE_PAD), lambda b, s: (idx(b, s), 0)),
            pl.BlockSpec((M_GATES, L), lambda b, s: (0, idx(b, s))),
        ]

    return pl.pallas_call(
        _scan_kernel,
        out_shape=(jax.ShapeDtypeStruct((N_ALL, MV), BF16),
                   jax.ShapeDtypeStruct((N_ALL, MV), BF16)),
        grid=(BATCH, steps),
        in_specs=specs(fwd) + specs(bwd),
        out_specs=(pl.BlockSpec((L, MV), lambda b, s: (fwd(b, s), 0)),
                   pl.BlockSpec((L, MV), lambda b, s: (bwd(b, s), 0))),
        scratch_shapes=[pltpu.VMEM((2 * MH, MDK, MDV + LANES), F32),
                        pltpu.VMEM((2 * MH, 8, LANES), F32)],
        compiler_params=_params(2),
        name="mlstm_scan",
    )(p, p, p, gc, gr, p, p, p, gc, gr)


def _mlstm_out_kernel(hf_ref, hb_ref, og_ref, x_ref, gp_ref, gt_ref, w_ref, out_ref, a_scr):
    for r0 in range(0, TM_MLSTM_OUT, TM_OUT):
        rows = slice(r0, r0 + TM_OUT)
        for h in range(MH):
            sl = slice(h * MDV, (h + 1) * MDV)
            hh = hf_ref[rows, sl].astype(F32) + hb_ref[rows, sl].astype(F32)
            r = lax.rsqrt(jnp.mean(hh * hh, axis=-1, keepdims=True) + EPS)
            a_scr[rows, sl] = ((hh * r) * og_ref[rows, sl].astype(F32)).astype(BF16)
        y = jnp.dot(a_scr[rows, :], w_ref[...], preferred_element_type=F32)
        out_ref[rows, :] = _post_norm_residual(x_ref[rows, :], y, gp_ref[...], gt_ref[...])


def _mlstm_out(hf, hb, p, x_all, g_post, mod, w_out):
    tm = TM_MLSTM_OUT
    vec = pl.BlockSpec((1, D), lambda i: (0, 0))
    return pl.pallas_call(
        _mlstm_out_kernel,
        out_shape=jax.ShapeDtypeStruct((N_ALL, D), F32),
        grid=(N_ALL // tm,),
        in_specs=[
            pl.BlockSpec((tm, MV), lambda i: (i, 0)),
            pl.BlockSpec((tm, MV), lambda i: (i, 0)),
            pl.BlockSpec((tm, MV), lambda i: (i, 2)),
            pl.BlockSpec((tm, D), lambda i: (i, 0)),
            vec,
            pl.BlockSpec((None, 1, D), lambda i: (_mod_row(i, tm), 0, 2)),
            pl.BlockSpec((MV, D), lambda i: (0, 0), pipeline_mode=pl.Buffered(1)),
        ],
        out_specs=pl.BlockSpec((tm, D), lambda i: (i, 0)),
        scratch_shapes=[pltpu.VMEM((tm, MV), BF16)],
        compiler_params=_params(1),
        name="mlstm_out",
    )(hf, hb, p, x_all, g_post, mod, w_out)


def _ffn_kernel(x_ref, g_ref, sh_ref, sc_ref, gt_ref, gp_ref, wg_ref, wu_ref, wo_ref, out_ref,
                h_scr):
    f = pl.program_id(1)
    last_f = pl.num_programs(1) - 1
    half = TM_FFN // 2

    def step(first, last):
        for r in range(0, TM_FFN, half):
            rows = slice(r, r + half)
            if first:
                h_scr[rows, :] = _adaln(x_ref[rows, :], g_ref[...], sh_ref[...],
                                        sc_ref[...]).astype(BF16)
            h = h_scr[rows, :]
            gate = jnp.dot(h, wg_ref[...], preferred_element_type=F32)
            up = jnp.dot(h, wu_ref[...], preferred_element_type=F32)
            act = ((gate * _sigmoid(gate)) * up).astype(BF16)
            y = jnp.dot(act, wo_ref[...], preferred_element_type=F32)
            if not first:
                y = out_ref[rows, :] + y
            if last:
                y = _post_norm_residual(x_ref[rows, :], y, gp_ref[...], gt_ref[...])
            out_ref[rows, :] = y

    pl.when(f == 0)(functools.partial(step, True, False))
    pl.when((f > 0) & (f < last_f))(functools.partial(step, False, False))
    pl.when(f == last_f)(functools.partial(step, False, True))


def _ffn(x_rows, g_pre, g_post, mod, w_in, w_out, layer):
    n_rows = x_rows.shape[0]
    tm, tf = TM_FFN, TF_FFN
    nf = FFN // tf
    assert nf >= 2
    vec = pl.BlockSpec((1, D), lambda i, f: (0, 0))
    return pl.pallas_call(
        _ffn_kernel,
        out_shape=jax.ShapeDtypeStruct((n_rows, D), F32),
        grid=(n_rows // tm, nf),
        in_specs=[
            pl.BlockSpec((tm, D), lambda i, f: (i, 0)),
            vec, _mod_spec(tm, 3), _mod_spec(tm, 4), _mod_spec(tm, 5), vec,
            pl.BlockSpec((None, D, tf), lambda i, f: (layer, 0, f)),
            pl.BlockSpec((None, D, tf), lambda i, f: (layer, 0, nf + f)),
            pl.BlockSpec((None, tf, D), lambda i, f: (layer, f, 0)),
        ],
        out_specs=pl.BlockSpec((tm, D), lambda i, f: (i, 0)),
        scratch_shapes=[pltpu.VMEM((tm, D), BF16)],
        compiler_params=_params(2),
        name="ffn",
    )(x_rows, g_pre, mod, mod, mod, g_post, w_in, w_in, w_out)


def _norm_rope_heads(y_ref, out_ref, n_cols, a, b):
    for c in range(0, n_cols, ADH):
        yy = y_ref[:, c:c + ADH]
        r = lax.rsqrt(jnp.mean(yy * yy, axis=-1, keepdims=True) + EPS)
        out_ref[:, c:c + ADH] = ((yy * a + pltpu.roll(yy, ADH // 2, 1) * b) * r).astype(BF16)


def _qkv_kernel(x_ref, g_ref, sh_ref, sc_ref, aq_ref, bq_ref, ak_ref, bk_ref, w_ref,
                q_ref, k_ref, vt_ref, h_scr, y_scr):
    j = pl.program_id(1)
    n_q = AH * ADH // TN_QKV
    kv_cols = AKV * ADH

    def project(jj):
        y_scr[jj % 2] = jnp.dot(h_scr[...], w_ref[...], preferred_element_type=F32)

    def finish(jj):
        y_ref = y_scr.at[jj % 2]
        if jj < n_q:
            _norm_rope_heads(y_ref, q_ref, TN_QKV, aq_ref[...], bq_ref[...])
        else:
            _norm_rope_heads(y_ref, k_ref, kv_cols, ak_ref[...], bk_ref[...])
            vt = y_ref[:, kv_cols:].T
            ones = jnp.ones((VT_ONES, vt.shape[1]), BF16)
            for hd in range(AKV):
                vt_ref[hd * VT_ROWS:hd * VT_ROWS + ADH, :] = (
                    vt[hd * ADH:(hd + 1) * ADH, :].astype(BF16))
                vt_ref[hd * VT_ROWS + ADH:(hd + 1) * VT_ROWS, :] = ones

    def step(jj):
        if jj == 0:
            h_scr[...] = _adaln(x_ref[...], g_ref[...], sh_ref[...], sc_ref[...]).astype(BF16)
        project(jj)
        if jj >= 1:
            finish(jj - 1)
        if jj == n_q:
            finish(jj)

    for jj in range(n_q + 1):
        pl.when(j == jj)(functools.partial(step, jj))


def _attn_qkv(x_all, gain, mod, a_q, b_q, a_k, b_k, w_qkv):
    tm, tn = TM_QKV, TN_QKV
    n_q = AH * ADH // tn
    n_lat = N_LAT // tm
    per_seq = SEQ // tm
    kv_cols = AKV * ADH
    assert tn == 2 * kv_cols
    vec = pl.BlockSpec((1, D), lambda i, j: (0, 0))
    rope = pl.BlockSpec((tm, ADH), lambda i, j: (jnp.where(i < n_lat, i % per_seq, per_seq), 0))
    return pl.pallas_call(
        _qkv_kernel,
        out_shape=(jax.ShapeDtypeStruct((N_ALL, AH * ADH), BF16),
                   jax.ShapeDtypeStruct((N_ALL, kv_cols), BF16),
                   jax.ShapeDtypeStruct((AKV * VT_ROWS, N_ALL), BF16)),
        grid=(N_ALL // tm, n_q + 1),
        in_specs=[
            pl.BlockSpec((tm, D), lambda i, j: (i, 0)),
            vec, _mod_spec(tm, 0), _mod_spec(tm, 1), rope, rope, rope, rope,
            pl.BlockSpec((D, tn), lambda i, j: (0, j)),
        ],
        out_specs=(pl.BlockSpec((tm, tn), lambda i, j: (i, jnp.clip(j - 1, 0, n_q - 1))),
                   pl.BlockSpec((tm, kv_cols), lambda i, j: (i, 0)),
                   pl.BlockSpec((AKV * VT_ROWS, tm), lambda i, j: (0, i))),
        scratch_shapes=[pltpu.VMEM((tm, D), BF16), pltpu.VMEM((2, tm, tn), F32)],
        compiler_params=_params(2),
        name="attn_qkv",
    )(x_all, gain, mod, mod, a_q, b_q, a_k, b_k, w_qkv)


def _attn_kernel(q_ref, kl_ref, kc_ref, vtl_ref, vtc_ref, o_ref):
    tq = q_ref.shape[0]
    nq = AG * tq
    q = jnp.concatenate([q_ref[:, g * ADH:(g + 1) * ADH] for g in range(AG)], axis=0)

    def block(k, vt, m, acc):
        st = lax.dot_general(k, q, _NT, preferred_element_type=F32)
        m_new = jnp.maximum(m, jnp.max(st, axis=0, keepdims=True))
        p = jnp.exp2(st - m_new).astype(BF16)
        acc = jnp.exp2(m - m_new) * acc + jnp.dot(vt, p, preferred_element_type=F32)
        return m_new, acc

    m = jnp.full((1, nq), -jnp.inf, F32)
    acc = jnp.zeros((VT_ROWS, nq), F32)
    for c in range(SEQ // TK_ATT):
        m, acc = block(kl_ref[c * TK_ATT:(c + 1) * TK_ATT, :],
                       vtl_ref[:, c * TK_ATT:(c + 1) * TK_ATT], m, acc)
    m, acc = block(kc_ref[...], vtc_ref[...], m, acc)
    out = (acc[:ADH, :] * (1.0 / acc[ADH:ADH + 1, :])).T
    for g in range(AG):
        o_ref[:, g * ADH:(g + 1) * ADH] = out[g * tq:(g + 1) * tq, :].astype(BF16)


def _attention(q, k, vt):
    tq = TQ_ATT
    per_seq = SEQ // tq
    return pl.pallas_call(
        _attn_kernel,
        out_shape=jax.ShapeDtypeStruct((N_LAT, AH * ADH), BF16),
        grid=(BATCH, AKV, per_seq),
        in_specs=[
            pl.BlockSpec((tq, AG * ADH), lambda b, h, i: (b * per_seq + i, h)),
            pl.BlockSpec((SEQ, ADH), lambda b, h, i: (b, h)),
            pl.BlockSpec((CTX, ADH), lambda b, h, i: (N_LAT // CTX + b, h)),
            pl.BlockSpec((VT_ROWS, SEQ), lambda b, h, i: (h, b)),
            pl.BlockSpec((VT_ROWS, CTX), lambda b, h, i: (h, N_LAT // CTX + b)),
        ],
        out_specs=pl.BlockSpec((tq, AG * ADH), lambda b, h, i: (b * per_seq + i, h)),
        compiler_params=_params(3),
        name="attention",
    )(q, k, k, vt, vt)


def _attn_out_kernel(o_ref, x_ref, gp_ref, gt_ref, w_ref, out_ref):
    for r in range(0, TM_ATT_OUT, TM_ATT_OUT // 2):
        rows = slice(r, r + TM_ATT_OUT // 2)
        y = jnp.dot(o_ref[rows, :], w_ref[...], preferred_element_type=F32)
        out_ref[rows, :] = _post_norm_residual(x_ref[rows, :], y, gp_ref[...], gt_ref[...])


def _attn_out(o, x_all, g_post, mod, w_out):
    tm = TM_ATT_OUT
    return pl.pallas_call(
        _attn_out_kernel,
        out_shape=jax.ShapeDtypeStruct((N_LAT, D), F32),
        grid=(N_LAT // tm,),
        in_specs=[
            pl.BlockSpec((tm, D), lambda i: (i, 0)),
            pl.BlockSpec((tm, D), lambda i: (i, 0)),
            pl.BlockSpec((1, D), lambda i: (0, 0)),
            pl.BlockSpec((None, 1, D), lambda i: (_mod_row(i, tm), 0, 2)),
            pl.BlockSpec((D, D), lambda i: (0, 0), pipeline_mode=pl.Buffered(1)),
        ],
        out_specs=pl.BlockSpec((tm, D), lambda i: (i, 0)),
        compiler_params=_params(1),
        name="attn_out",
    )(o, x_all, g_post, mod, w_out)


def _half_split(a):
    lead = a.shape[:-1]
    n = a.shape[-1] // ADH
    a = a.reshape(*lead, n, ADH // 2, 2)
    return jnp.swapaxes(a, -1, -2).reshape(*lead, n * ADH)


def _qkv_weight_kernel(w_ref, o_ref):
    first_head = pl.program_id(0) * (TN_WPERM // ADH)
    src = lax.broadcasted_iota(jnp.int32, (ADH, ADH), 0)
    dst = lax.broadcasted_iota(jnp.int32, (ADH, ADH), 1)
    half = ADH // 2
    split_src = jnp.where(dst < half, 2 * dst, 2 * (dst - half) + 1)
    for h in range(TN_WPERM // ADH):
        is_qk = first_head + h < AH + AKV
        perm = (src == jnp.where(is_qk, split_src, dst)).astype(BF16)
        cols = slice(h * ADH, (h + 1) * ADH)
        o_ref[:, cols] = jnp.dot(w_ref[:, cols].astype(BF16), perm,
                                 preferred_element_type=F32).astype(BF16)


def _qkv_weight(w_qkv):
    n_cols = w_qkv.shape[1]
    return pl.pallas_call(
        _qkv_weight_kernel,
        out_shape=jax.ShapeDtypeStruct((D, n_cols), BF16),
        grid=(n_cols // TN_WPERM,),
        in_specs=[pl.BlockSpec((D, TN_WPERM), lambda j: (0, j))],
        out_specs=pl.BlockSpec((D, TN_WPERM), lambda j: (0, j)),
        compiler_params=_params(1),
        name="qkv_weight",
    )(w_qkv)


def _rope_tables(gain, scale):
    rows = SEQ // GRID_W
    t_row = jnp.repeat(jnp.arange(rows, dtype=F32), GRID_W)
    t_col = jnp.tile(jnp.arange(GRID_W, dtype=F32), rows)
    per_axis = ADH // 2
    inv = ROPE_THETA ** (-jnp.arange(0, per_axis, 2, dtype=F32) / per_axis)
    ang = jnp.concatenate([t_row[:, None] * inv, t_col[:, None] * inv], axis=-1)
    ang = jnp.concatenate([ang, jnp.zeros((TM_QKV, per_axis), F32)], axis=0)
    cos = jnp.concatenate([jnp.cos(ang), jnp.cos(ang)], axis=-1)
    sin = jnp.concatenate([-jnp.sin(ang), jnp.sin(ang)], axis=-1)
    g = _half_split(gain.astype(F32))
    return (g * scale) * cos, (jnp.roll(g, per_axis) * scale) * sin


def kernel(x, c, ctx, c_ctx, w_mod, b_mod, g_mix_pre, g_mix_post, g_ffn_pre, g_ffn_post,
           w_mlstm_in, b_mlstm_gate, g_mlstm_head, w_mlstm_out,
           w_attn_qkv, g_attn_q, g_attn_k, w_attn_out, w_ffn_in, w_ffn_out):
    assert x.shape == (BATCH, SEQ, D) and ctx.shape == (BATCH, CTX, D)
    xl = x.reshape(N_LAT, D)
    xc = ctx.reshape(N_CTX, D)
    row = lambda a: a.reshape(1, -1)

    c_all = jnp.concatenate([c, c_ctx[None, :], jnp.zeros((MOD_ROWS - BATCH - 1, D), F32)], axis=0)
    mod = _modulation(c_all, w_mod, b_mod).reshape(2, MOD_ROWS, 1, N_MOD * D)

    w_main = w_mlstm_in[0].astype(BF16)
    w_gate = jnp.pad(w_main[:, M_MAIN:], ((0, 0), (0, GATE_PAD - M_GATES)))
    b_gate = jnp.pad(b_mlstm_gate[0], (0, GATE_PAD - M_GATES)).reshape(1, GATE_PAD)
    p, gpre, x_all = _mlstm_inproj(xl, xc, row(g_mix_pre[0]), mod[0], w_main, w_gate,
                                   row(g_mlstm_head[0]))
    gc, gr = _gateprep(gpre, b_gate)
    hf, hb = _mlstm_scan(p, gc, gr)
    x_all = _mlstm_out(hf, hb, p, x_all, row(g_mix_post[0]), mod[0], w_mlstm_out[0].astype(BF16))
    w_ffn_in16 = w_ffn_in.astype(BF16)
    w_ffn_out16 = w_ffn_out.astype(BF16)
    x_all = _ffn(x_all, row(g_ffn_pre[0]), row(g_ffn_post[0]), mod[0], w_ffn_in16, w_ffn_out16, 0)

    w_qkv = _qkv_weight(w_attn_qkv[0])
    a_q, b_q = _rope_tables(g_attn_q[0], A_SCALE * LOG2E)
    a_k, b_k = _rope_tables(g_attn_k[0], 1.0)
    q, k, vt = _attn_qkv(x_all, row(g_mix_pre[1]), mod[1], a_q, b_q, a_k, b_k, w_qkv)
    o = _attention(q, k, vt)
    x_lat = _attn_out(o, x_all, row(g_mix_post[1]), mod[1], w_attn_out[0].astype(BF16))
    x_lat = _ffn(x_lat, row(g_ffn_pre[1]), row(g_ffn_post[1]), mod[1], w_ffn_in16, w_ffn_out16, 1)
    return x_lat.reshape(BATCH, SEQ, D)
```
